```python
import jax, jax.numpy as jnp
from jax import lax
import numpy as np

D_MODEL = 1024
BATCH = 8
SEQ = 2048
DEPTH = 2

MIX_WIDTH = D_MODEL
DN_HEADS = 4
DN_HEAD_K = 128
DN_HEAD_V = 128
DN_K_WIDTH = DN_HEADS * DN_HEAD_K
DN_V_WIDTH = DN_HEADS * DN_HEAD_V
QKV_DIM = 2 * DN_K_WIDTH + DN_V_WIDTH
CONV_WIDTH = 4
DN_CHUNK = 64
GM_GROUPS = 4
GM_GROUP_DIM = 128
GM_WIDTH = GM_GROUPS * GM_GROUP_DIM
GM_CHUNK = 128
D_FF = -(-8 * D_MODEL // (3 * 256)) * 256
EPS = 1e-6

Q_OFF = 0
K_OFF = Q_OFF + DN_K_WIDTH
V_OFF = K_OFF + DN_K_WIDTH
Z_OFF = V_OFF + DN_V_WIDTH
BETA_OFF = Z_OFF + DN_V_WIDTH
A_OFF = BETA_OFF + DN_HEADS
GM_OFF = A_OFF + DN_HEADS
IN_DIM = GM_OFF + 2 * GM_WIDTH

kernel_name = "hybrid_gdn_gmlp_parallel_heads"


def _rmsnorm(x, g):
    xf = x.astype(jnp.float32)
    y = xf * lax.rsqrt(jnp.mean(xf * xf, axis=-1, keepdims=True) + EPS)
    return (y * g.astype(jnp.float32)).astype(x.dtype)


def _l2norm(x):
    return x * lax.rsqrt(jnp.sum(x * x, axis=-1, keepdims=True) + EPS)


def _causal_depthwise_conv(x, w):
    c = x.shape[-1]
    return lax.conv_general_dilated(
        x, w[:, None, :].astype(x.dtype), window_strides=(1,),
        padding=((CONV_WIDTH - 1, 0),), dimension_numbers=("NWC", "WIO", "NWC"),
        feature_group_count=c)


def _gated_delta_rule(q, k, v, g, beta):
    b, t, h, dk = q.shape
    dv = v.shape[-1]
    n = t // DN_CHUNK

    def chunks(a):
        a = jnp.moveaxis(a, 2, 1)
        return a.reshape((b, h, n, DN_CHUNK) + a.shape[3:])

    q, k, v, g, beta = (chunks(a) for a in (q, k, v, g, beta))
    g = jnp.cumsum(g, axis=-1)
    idx = jnp.arange(DN_CHUNK)
    incl = idx[:, None] >= idx[None, :]
    strict = idx[:, None] > idx[None, :]
    diff = g[..., :, None] - g[..., None, :]
    decay = jnp.where(incl, jnp.exp(jnp.where(incl, diff, 0.0)), 0.0)
    k_beta = k * beta[..., None]
    v_beta = v * beta[..., None]
    a_mat = jnp.where(strict, jnp.einsum("bhncd,bhnsd->bhncs", k_beta, k) * decay, 0.0)
    eye = jnp.eye(DN_CHUNK, dtype=a_mat.dtype)
    tri = a_mat + eye
    u = lax.linalg.triangular_solve(tri, v_beta, left_side=True, lower=True)
    w = lax.linalg.triangular_solve(tri, k_beta * jnp.exp(g)[..., None], left_side=True, lower=True)
    qk = jnp.einsum("bhncd,bhnsd->bhncs", q, k) * decay

    def step(s, xs):
        q_c, k_c, u_c, w_c, g_c, qk_c = xs
        v_new = u_c - jnp.einsum("bhcd,bhde->bhce", w_c, s)
        o = (jnp.einsum("bhcd,bhde->bhce", q_c * jnp.exp(g_c)[..., None], s)
             + jnp.einsum("bhcs,bhse->bhce", qk_c, v_new))
        g_last = g_c[..., -1]
        s = (s * jnp.exp(g_last)[..., None, None]
             + jnp.einsum("bhcd,bhce->bhde", k_c * jnp.exp(g_last[..., None] - g_c)[..., None], v_new))
        return s, o

    xs = tuple(jnp.moveaxis(a, 2, 0) for a in (q, k, u, w, g, qk))
    s0 = jnp.zeros((b, h, dk, dv), jnp.float32)
    _, o = lax.scan(step, s0, xs)
    o = jnp.moveaxis(o, 0, 2).reshape(b, h, t, dv)
    return jnp.moveaxis(o, 1, 2)


def _hybrid_mixer(hn, w_in, conv_w, a_log, dt_bias, o_norm_g, ln_v_g, ln_v_b, w_s, b_s, w_out):
    b, t, _ = hn.shape
    proj = hn @ w_in

    qkv = jax.nn.silu(_causal_depthwise_conv(proj[..., Q_OFF:Z_OFF], conv_w)).astype(jnp.float32)
    q = _l2norm(qkv[..., Q_OFF:K_OFF].reshape(b, t, DN_HEADS, DN_HEAD_K)) * (DN_HEAD_K ** -0.5)
    k = _l2norm(qkv[..., K_OFF:V_OFF].reshape(b, t, DN_HEADS, DN_HEAD_K))
    v = qkv[..., V_OFF:Z_OFF].reshape(b, t, DN_HEADS, DN_HEAD_V)
    z = proj[..., Z_OFF:BETA_OFF].astype(jnp.float32).reshape(b, t, DN_HEADS, DN_HEAD_V)
    beta = jax.nn.sigmoid(proj[..., BETA_OFF:A_OFF].astype(jnp.float32))
    g = -jnp.exp(a_log.astype(jnp.float32)) * jax.nn.softplus(
        proj[..., A_OFF:GM_OFF].astype(jnp.float32) + dt_bias.astype(jnp.float32))
    o = _gated_delta_rule(q, k, v, g, beta)
    o = o * lax.rsqrt(jnp.mean(o * o, axis=-1, keepdims=True) + EPS)
    o = o * o_norm_g.astype(jnp.float32) * jax.nn.silu(z)
    o_dn = o.reshape(b, t, DN_V_WIDTH).astype(hn.dtype)

    gm = jax.nn.gelu(proj[..., GM_OFF:IN_DIM])
    u_g = gm[..., :GM_WIDTH]
    v_g = gm[..., GM_WIDTH:].astype(jnp.float32).reshape(b, t, GM_GROUPS, GM_GROUP_DIM)
    mu = jnp.mean(v_g, axis=-1, keepdims=True)
    var = jnp.mean(jnp.square(v_g - mu), axis=-1, keepdims=True)
    v_g = ((v_g - mu) * lax.rsqrt(var + EPS) * ln_v_g.reshape(GM_GROUPS, GM_GROUP_DIM)
           + ln_v_b.reshape(GM_GROUPS, GM_GROUP_DIM)).astype(hn.dtype)
    v_g = v_g.reshape(b, t // GM_CHUNK, GM_CHUNK, GM_GROUPS, GM_GROUP_DIM)
    pos = jnp.arange(GM_CHUNK)
    ws = jnp.where(pos[:, None] >= pos[None, :], w_s, 0.0).astype(hn.dtype)
    sp = jnp.einsum("gts,bnsgc->bntgc", ws, v_g) + b_s.T[:, :, None].astype(hn.dtype)
    o_gm = u_g * sp.reshape(b, t, GM_WIDTH)

    return jnp.concatenate([o_dn, o_gm], axis=-1) @ w_out


def _swiglu(h, w_gate, w_up, w_down):
    return (jax.nn.silu(h @ w_gate) * (h @ w_up)) @ w_down


def setup_inputs(seed: int = 0) -> dict:
    key = jax.random.key(seed)
    ks = jax.random.split(key, 20)
    f32 = jnp.float32
    nrm = lambda k, shape, scale: jax.random.normal(k, shape, f32) * scale
    x = jax.random.normal(ks[0], (BATCH, SEQ, D_MODEL), f32)
    norm_mix = 1.0 + nrm(ks[1], (DEPTH, D_MODEL), 0.02)
    w_in = nrm(ks[2], (DEPTH, D_MODEL, IN_DIM), D_MODEL ** -0.5)
    conv_w = nrm(ks[3], (DEPTH, CONV_WIDTH, QKV_DIM), CONV_WIDTH ** -0.5)
    a_log = jnp.log(jax.random.uniform(ks[4], (DEPTH, DN_HEADS), f32, 1.0, 16.0))
    dt = jnp.exp(jax.random.uniform(ks[5], (DEPTH, DN_HEADS), f32, np.log(1e-3), np.log(1e-1)))
    dt_bias = dt + jnp.log(-jnp.expm1(-dt))
    o_norm_g = 1.0 + nrm(ks[6], (DEPTH, DN_HEAD_V), 0.02)
    ln_v_g = 1.0 + nrm(ks[7], (DEPTH, GM_WIDTH), 0.02)
    ln_v_b = nrm(ks[8], (DEPTH, GM_WIDTH), 0.02)
    w_s = nrm(ks[9], (DEPTH, GM_GROUPS, GM_CHUNK, GM_CHUNK), GM_CHUNK ** -0.5)
    b_s = 1.0 + nrm(ks[10], (DEPTH, GM_GROUPS, GM_CHUNK), 0.02)
    w_out = nrm(ks[11], (DEPTH, MIX_WIDTH, D_MODEL), MIX_WIDTH ** -0.5)
    norm_ffn = 1.0 + nrm(ks[12], (DEPTH, D_MODEL), 0.02)
    w_gate = nrm(ks[13], (DEPTH, D_MODEL, D_FF), D_MODEL ** -0.5)
    w_up = nrm(ks[14], (DEPTH, D_MODEL, D_FF), D_MODEL ** -0.5)
    w_down = nrm(ks[15], (DEPTH, D_FF, D_MODEL), D_FF ** -0.5)
    norm_final = 1.0 + nrm(ks[16], (D_MODEL,), 0.02)
    return {"x": x, "norm_mix": norm_mix, "w_in": w_in, "conv_w": conv_w,
            "a_log": a_log, "dt_bias": dt_bias, "o_norm_g": o_norm_g,
            "ln_v_g": ln_v_g, "ln_v_b": ln_v_b, "w_s": w_s, "b_s": b_s,
            "w_out": w_out, "norm_ffn": norm_ffn, "w_gate": w_gate, "w_up": w_up,
            "w_down": w_down, "norm_final": norm_final}


def reference(x, norm_mix, w_in, conv_w, a_log, dt_bias, o_norm_g, ln_v_g, ln_v_b,
              w_s, b_s, w_out, norm_ffn, w_gate, w_up, w_down, norm_final):
    h = x
    for l in range(DEPTH):
        hn = _rmsnorm(h, norm_mix[l])
        h = h + _hybrid_mixer(hn, w_in[l], conv_w[l], a_log[l], dt_bias[l], o_norm_g[l],
                              ln_v_g[l], ln_v_b[l], w_s[l], b_s[l], w_out[l])
        h = h + _swiglu(_rmsnorm(h, norm_ffn[l]), w_gate[l], w_up[l], w_down[l])
    return _rmsnorm(h, norm_final)
```

```python
import functools

import jax
import jax.numpy as jnp
from jax import lax
from jax.experimental import pallas as pl
from jax.experimental.pallas import tpu as pltpu

D_MODEL = 1024
DN_HEADS = 4
DN_HEAD = 128
DN_WIDTH = DN_HEADS * DN_HEAD
QKV_DIM = 3 * DN_WIDTH
CONV_WIDTH = 4
DN_CHUNK = 64
GM_GROUPS = 4
GM_GROUP_DIM = 128
GM_WIDTH = GM_GROUPS * GM_GROUP_DIM
GM_CHUNK = 128
D_FF = 2816
EPS = 1e-6

Z_OFF = QKV_DIM
BETA_OFF = Z_OFF + DN_WIDTH
GM_OFF = BETA_OFF + 2 * DN_HEADS
IN_DIM = GM_OFF + 2 * GM_WIDTH

LANES = 128
SUBLANES = 8

P_Z = QKV_DIM
P_BA = P_Z + DN_WIDTH
P_GM = P_BA + LANES
P_DIM = P_GM + 2 * GM_WIDTH

SEQ_TILE = GM_CHUNK
BATCH_TILE = 4
HALO = SUBLANES
FFN_ROWS = 512
VMEM_LIMIT_BYTES = 56 * 1024 * 1024


def _dot(a, b):
    return jnp.dot(a, b, preferred_element_type=jnp.float32)


def _dot_nt(a, b):
    return lax.dot_general(a, b, (((1,), (1,)), ((), ())), preferred_element_type=jnp.float32)


def _dot_tn(a, b):
    return lax.dot_general(a, b, (((0,), (0,)), ((), ())), preferred_element_type=jnp.float32)


def _sigmoid(x):
    return 1.0 / (1.0 + jnp.exp(-x))


def _silu(x):
    return x * _sigmoid(x)


def _gelu_tanh(x):
    return 0.5 * x * (1.0 + jnp.tanh(0.7978845608028654 * (x + 0.044715 * (x * x * x))))


def _softplus(x):
    return jnp.maximum(x, 0.0) + jnp.log1p(jnp.exp(-jnp.abs(x)))


def _bf16(x):
    return x.astype(jnp.bfloat16)


def _mixer_kernel(x_ref, nrm_ref, w_ref, cw_ref, alog_ref, dtb_ref, ong_ref, lng_ref, lnb_ref,
                  ws_ref, bs_ref, wo_ref, out_ref,
                  pre_ref, qkv_ref, z_ref, ug_ref, vln_ref, gate_ref, o_ref, s_ref):
    nb = x_ref.shape[0]
    m = nb * SEQ_TILE
    t_idx = pl.program_id(1)

    @pl.when(t_idx == 0)
    def _():
        s_ref[...] = jnp.zeros_like(s_ref)
        pre_ref[:, 0:HALO, :] = jnp.zeros((nb, HALO, QKV_DIM), jnp.float32)

    x = x_ref[...].reshape(m, D_MODEL)
    xn = x * lax.rsqrt(jnp.mean(x * x, axis=-1, keepdims=True) + EPS) * nrm_ref[...]
    xn = _bf16(xn)

    pre_ref[:, HALO:HALO + SEQ_TILE, :] = _dot(xn, w_ref[:, 0:QKV_DIM]).reshape(nb, SEQ_TILE, QKV_DIM)
    z_ref[...] = _dot(xn, w_ref[:, P_Z:P_BA])

    ba = _dot(xn, w_ref[:, P_BA:P_GM])
    lane = lax.broadcasted_iota(jnp.int32, (m, LANES), 1)
    row = lax.broadcasted_iota(jnp.int32, (m, LANES), 0)
    g = -jnp.exp(alog_ref[...]) * _softplus(ba + dtb_ref[...])
    pos = row % DN_CHUNK
    shift = 1
    while shift < DN_CHUNK:
        g = g + jnp.where(pos >= shift, pltpu.roll(g, shift, axis=0), 0.0)
        shift *= 2
    gate_ref[...] = jnp.where(lane < DN_HEADS, _sigmoid(ba), g)

    for j in range(2 * GM_GROUPS):
        gm = _gelu_tanh(_dot(xn, w_ref[:, P_GM + j * LANES:P_GM + (j + 1) * LANES]))
        if j < GM_GROUPS:
            ug_ref[:, j * LANES:(j + 1) * LANES] = gm
        else:
            c0 = (j - GM_GROUPS) * LANES
            mu = jnp.mean(gm, axis=-1, keepdims=True)
            d = gm - mu
            var = jnp.mean(d * d, axis=-1, keepdims=True)
            vln_ref[:, c0:c0 + LANES] = (d * lax.rsqrt(var + EPS) * lng_ref[:, c0:c0 + LANES]
                                         + lnb_ref[:, c0:c0 + LANES])

    for j in range(QKV_DIM // LANES):
        c0 = j * LANES
        acc = pre_ref[:, HALO:HALO + SEQ_TILE, c0:c0 + LANES] * cw_ref[CONV_WIDTH - 1:CONV_WIDTH, c0:c0 + LANES]
        for s in range(1, CONV_WIDTH):
            acc = acc + (pre_ref[:, HALO - s:HALO - s + SEQ_TILE, c0:c0 + LANES]
                         * cw_ref[CONV_WIDTH - 1 - s:CONV_WIDTH - s, c0:c0 + LANES])
        y = _silu(acc).reshape(m, LANES)
        if j < 2 * DN_HEADS:
            y = y * lax.rsqrt(jnp.sum(y * y, axis=-1, keepdims=True) + EPS)
            if j < DN_HEADS:
                y = y * (DN_HEAD ** -0.5)
        qkv_ref[:, c0:c0 + LANES] = y
    pre_ref[:, 0:HALO, :] = pre_ref[:, SEQ_TILE:SEQ_TILE + HALO, :]

    ii = lax.broadcasted_iota(jnp.int32, (DN_CHUNK, DN_CHUNK), 0)
    jj = lax.broadcasted_iota(jnp.int32, (DN_CHUNK, DN_CHUNK), 1)
    incl = ii >= jj
    strict = ii > jj
    eye = jnp.where(ii == jj, 1.0, 0.0).astype(jnp.float32)
    ti = lax.broadcasted_iota(jnp.int32, (GM_CHUNK, GM_CHUNK), 0)
    tj = lax.broadcasted_iota(jnp.int32, (GM_CHUNK, GM_CHUNK), 1)
    ws_causal = [_bf16(jnp.where(ti >= tj, ws_ref[gi], 0.0)) for gi in range(GM_GROUPS)]

    def per_batch(b, carry):
        r0 = pl.multiple_of(b * SEQ_TILE, SEQ_TILE)
        gates = gate_ref[pl.ds(r0, SEQ_TILE), :]
        gates_t = gates.T
        for c in range(SEQ_TILE // DN_CHUNK):
            rc = r0 + c * DN_CHUNK
            t0 = c * DN_CHUNK
            for h in range(DN_HEADS):
                qc = qkv_ref[pl.ds(rc, DN_CHUNK), h * DN_HEAD:(h + 1) * DN_HEAD]
                kc = qkv_ref[pl.ds(rc, DN_CHUNK), DN_WIDTH + h * DN_HEAD:DN_WIDTH + (h + 1) * DN_HEAD]
                vc = qkv_ref[pl.ds(rc, DN_CHUNK), 2 * DN_WIDTH + h * DN_HEAD:2 * DN_WIDTH + (h + 1) * DN_HEAD]
                g_col = gates[t0:t0 + DN_CHUNK, DN_HEADS + h:DN_HEADS + h + 1]
                b_col = gates[t0:t0 + DN_CHUNK, h:h + 1]
                g_row = gates_t[DN_HEADS + h:DN_HEADS + h + 1, t0:t0 + DN_CHUNK]
                b_row = gates_t[h:h + 1, t0:t0 + DN_CHUNK]

                decay = jnp.where(incl, jnp.exp(jnp.where(incl, g_col - g_row, 0.0)), 0.0)
                kc16 = _bf16(kc)
                kq = _dot_nt(jnp.concatenate([kc16, _bf16(qc)], axis=0), kc16)
                a_mat = jnp.where(strict, kq[0:DN_CHUNK] * b_col * decay, 0.0)
                qk = kq[DN_CHUNK:2 * DN_CHUNK] * decay

                inv = eye - a_mat
                pw = a_mat
                for _ in range(5):
                    pw16 = _bf16(pw)
                    pw = _dot(pw16, pw16)
                    inv = inv + _dot(_bf16(inv), _bf16(pw))
                inv_b = inv * b_row
                u = _dot(_bf16(inv_b), _bf16(vc))
                w = _dot(_bf16(inv_b * jnp.exp(g_row)), kc16)

                s_old = s_ref[b * DN_HEADS + h]
                wq = _dot(_bf16(jnp.concatenate([w, qc * jnp.exp(g_col)], axis=0)), _bf16(s_old))
                v_new = u - wq[0:DN_CHUNK]
                v_new16 = _bf16(v_new)
                o = wq[DN_CHUNK:2 * DN_CHUNK] + _dot(_bf16(qk), v_new16)
                g_last = g_col[DN_CHUNK - 1:DN_CHUNK, :]
                kd = kc * jnp.exp(g_last - g_col)
                s_ref[b * DN_HEADS + h] = s_old * jnp.exp(g_last) + _dot_tn(_bf16(kd), v_new16)

                o = o * lax.rsqrt(jnp.mean(o * o, axis=-1, keepdims=True) + EPS)
                zc = z_ref[pl.ds(rc, DN_CHUNK), h * DN_HEAD:(h + 1) * DN_HEAD]
                o = o * ong_ref[...] * _silu(zc)
                o_ref[pl.ds(rc, DN_CHUNK), h * DN_HEAD:(h + 1) * DN_HEAD] = _bf16(o)

        for gi in range(GM_GROUPS):
            c0 = gi * GM_GROUP_DIM
            vl = vln_ref[pl.ds(r0, SEQ_TILE), c0:c0 + GM_GROUP_DIM]
            sp = _dot(ws_causal[gi], _bf16(vl)) + bs_ref[gi]
            o_ref[pl.ds(r0, SEQ_TILE), DN_WIDTH + c0:DN_WIDTH + c0 + GM_GROUP_DIM] = _bf16(
                ug_ref[pl.ds(r0, SEQ_TILE), c0:c0 + GM_GROUP_DIM] * sp)
        return carry

    lax.fori_loop(0, nb, per_batch, 0)

    y = _dot(o_ref[...], wo_ref[...])
    out_ref[...] = (x + y).reshape(nb, SEQ_TILE, D_MODEL)


def _mixer(h, nrm, w_pack, cw, alog, dtb, ong, lng, lnb, ws, bs_b, wo):
    batch, seq, _ = h.shape
    nb = BATCH_TILE
    m = nb * SEQ_TILE
    const2 = lambda bi, ti: (0, 0)
    const3 = lambda bi, ti: (0, 0, 0)
    return pl.pallas_call(
        _mixer_kernel,
        out_shape=jax.ShapeDtypeStruct(h.shape, h.dtype),
        grid=(batch // nb, seq // SEQ_TILE),
        in_specs=[
            pl.BlockSpec((nb, SEQ_TILE, D_MODEL), lambda bi, ti: (bi, ti, 0)),
            pl.BlockSpec((1, D_MODEL), const2),
            pl.BlockSpec((D_MODEL, P_DIM), const2),
            pl.BlockSpec((CONV_WIDTH, QKV_DIM), const2),
            pl.BlockSpec((1, LANES), const2),
            pl.BlockSpec((1, LANES), const2),
            pl.BlockSpec((1, DN_HEAD), const2),
            pl.BlockSpec((1, GM_WIDTH), const2),
            pl.BlockSpec((1, GM_WIDTH), const2),
            pl.BlockSpec((GM_GROUPS, GM_CHUNK, GM_CHUNK), const3),
            pl.BlockSpec((GM_GROUPS, GM_CHUNK, GM_GROUP_DIM), const3),
            pl.BlockSpec((D_MODEL, D_MODEL), const2),
        ],
        out_specs=pl.BlockSpec((nb, SEQ_TILE, D_MODEL), lambda bi, ti: (bi, ti, 0)),
        scratch_shapes=[
            pltpu.VMEM((nb, HALO + SEQ_TILE, QKV_DIM), jnp.float32),
            pltpu.VMEM((m, QKV_DIM), jnp.float32),
            pltpu.VMEM((m, DN_WIDTH), jnp.float32),
            pltpu.VMEM((m, GM_WIDTH), jnp.float32),
            pltpu.VMEM((m, GM_WIDTH), jnp.float32),
            pltpu.VMEM((m, LANES), jnp.float32),
            pltpu.VMEM((m, D_MODEL), jnp.bfloat16),
            pltpu.VMEM((nb * DN_HEADS, DN_HEAD, DN_HEAD), jnp.float32),
        ],
        compiler_params=pltpu.CompilerParams(
            dimension_semantics=("arbitrary", "arbitrary"),
            vmem_limit_bytes=VMEM_LIMIT_BYTES),
        name="mixer",
    )(h, nrm, w_pack, cw, alog, dtb, ong, lng, lnb, ws, bs_b, wo)


def _ffn_kernel(x_ref, nrm_ref, wg_ref, wu_ref, wd_ref, fin_ref, out_ref, *, final_norm):
    x = x_ref[...]
    xn = _bf16(x * lax.rsqrt(jnp.mean(x * x, axis=-1, keepdims=True) + EPS) * nrm_ref[...])
    act = _bf16(_silu(_dot(xn, wg_ref[...])) * _dot(xn, wu_ref[...]))
    y = x + _dot(act, wd_ref[...])
    if final_norm:
        y = y * lax.rsqrt(jnp.mean(y * y, axis=-1, keepdims=True) + EPS) * fin_ref[...]
    out_ref[...] = y


def _ffn(h2, nrm, wg, wu, wd, fin, final_norm):
    n = h2.shape[0]
    const2 = lambda i: (0, 0)
    return pl.pallas_call(
        functools.partial(_ffn_kernel, final_norm=final_norm),
        out_shape=jax.ShapeDtypeStruct(h2.shape, h2.dtype),
        grid=(n // FFN_ROWS,),
        in_specs=[
            pl.BlockSpec((FFN_ROWS, D_MODEL), lambda i: (i, 0)),
            pl.BlockSpec((1, D_MODEL), const2),
            pl.BlockSpec((D_MODEL, D_FF), const2, pipeline_mode=pl.Buffered(1)),
            pl.BlockSpec((D_MODEL, D_FF), const2, pipeline_mode=pl.Buffered(1)),
            pl.BlockSpec((D_FF, D_MODEL), const2, pipeline_mode=pl.Buffered(1)),
            pl.BlockSpec((1, D_MODEL), const2),
        ],
        out_specs=pl.BlockSpec((FFN_ROWS, D_MODEL), lambda i: (i, 0)),
        compiler_params=pltpu.CompilerParams(
            dimension_semantics=("arbitrary",),
            vmem_limit_bytes=VMEM_LIMIT_BYTES),
        name="ffn",
    )(h2, nrm, wg, wu, wd, fin)


def _pad_lanes(v, offset):
    return jnp.zeros((1, LANES), jnp.float32).at[0, offset:offset + v.shape[0]].set(v)


def kernel(x, norm_mix, w_in, conv_w, a_log, dt_bias, o_norm_g, ln_v_g, ln_v_b, w_s, b_s, w_out,
           norm_ffn, w_gate, w_up, w_down, norm_final):
    batch, seq, d = x.shape
    depth = w_in.shape[0]
    h = x
    for l in range(depth):
        w_ba = jnp.pad(w_in[l][:, BETA_OFF:GM_OFF], ((0, 0), (0, LANES - 2 * DN_HEADS)))
        w_pack = _bf16(jnp.concatenate([w_in[l][:, :BETA_OFF], w_ba, w_in[l][:, GM_OFF:]], axis=1))
        bs_b = jnp.broadcast_to(b_s[l][:, :, None], (GM_GROUPS, GM_CHUNK, GM_GROUP_DIM))
        h = _mixer(h, norm_mix[l][None, :], w_pack, conv_w[l],
                   _pad_lanes(a_log[l], DN_HEADS), _pad_lanes(dt_bias[l], DN_HEADS),
                   o_norm_g[l][None, :], ln_v_g[l][None, :], ln_v_b[l][None, :],
                   w_s[l], bs_b, _bf16(w_out[l]))
        h = _ffn(h.reshape(batch * seq, d), norm_ffn[l][None, :], _bf16(w_gate[l]), _bf16(w_up[l]),
                 _bf16(w_down[l]), norm_final[None, :], l == depth - 1).reshape(batch, seq, d)
    return h
```

```python
import functools

import jax
import jax.numpy as jnp
from jax import lax
from jax.experimental import pallas as pl
from jax.experimental.pallas import tpu as pltpu

D_MODEL = 1024
DN_HEADS = 4
DN_HEAD = 128
DN_WIDTH = DN_HEADS * DN_HEAD
QKV_DIM = 3 * DN_WIDTH
CONV_WIDTH = 4
DN_CHUNK = 64
GM_GROUPS = 4
GM_GROUP_DIM = 128
GM_WIDTH = GM_GROUPS * GM_GROUP_DIM
GM_CHUNK = 128
D_FF = 2816
EPS = 1e-6

Z_OFF = QKV_DIM
BETA_OFF = Z_OFF + DN_WIDTH
GM_OFF = BETA_OFF + 2 * DN_HEADS
IN_DIM = GM_OFF + 2 * GM_WIDTH

LANES = 128
SUBLANES = 8

P_Z = QKV_DIM
P_BA = P_Z + DN_WIDTH
P_GM = P_BA + LANES
P_DIM = P_GM + 2 * GM_WIDTH

SEQ_TILE = GM_CHUNK
BATCH_TILE = 4
HALO = SUBLANES
FFN_ROWS = 512
VMEM_LIMIT_BYTES = 56 * 1024 * 1024

CHUNKS_PER_TILE = SEQ_TILE // DN_CHUNK
HEAD_PAIRS = DN_HEADS // 2
PAIR_WIDTH = 2 * DN_HEAD
MAT_WIDTH = DN_HEADS * DN_CHUNK
CHUNK_SHIFT = 6
HEAD_SHIFT = 7


def _dot(a, b):
    return jnp.dot(a, b, preferred_element_type=jnp.float32)


def _dot_nt(a, b):
    return lax.dot_general(a, b, (((1,), (1,)), ((), ())), preferred_element_type=jnp.float32)


def _dot_tn(a, b):
    return lax.dot_general(a, b, (((0,), (0,)), ((), ())), preferred_element_type=jnp.float32)


def _sigmoid(x):
    return 1.0 / (1.0 + jnp.exp(-x))


def _silu(x):
    return x * _sigmoid(x)


def _gelu_tanh(x):
    return 0.5 * x * (1.0 + jnp.tanh(0.7978845608028654 * (x + 0.044715 * (x * x * x))))


def _softplus(x):
    return jnp.maximum(x, 0.0) + jnp.log1p(jnp.exp(-jnp.abs(x)))


def _bf16(x):
    return x.astype(jnp.bfloat16)


def _split3(x):
    hi = _bf16(x)
    r1 = x - hi.astype(jnp.float32)
    mid = _bf16(r1)
    lo = _bf16(r1 - mid.astype(jnp.float32))
    return hi, mid, lo


def _select_right(x, sel):
    hi, mid, lo = _split3(x)
    return (_dot(lo, sel) + _dot(mid, sel)) + _dot(hi, sel)


def _select_left(sel, x):
    hi, mid, lo = _split3(x)
    return (_dot(sel, lo) + _dot(sel, mid)) + _dot(sel, hi)


def _iota(shape, dim):
    return lax.broadcasted_iota(jnp.int32, shape, dim)


def _tile_rows(y, n):
    return jnp.concatenate([y] * n, axis=0)


def _mixer_kernel(x_ref, nrm_ref, w_ref, cw_ref, alog_ref, dtb_ref, ong_ref, lng_ref, lnb_ref,
                  ws_ref, bs_ref, wo_ref, out_ref,
                  pre_ref, qkv_ref, z_ref, ug_ref, vln_ref, o_ref, sp_ref):
    nb = x_ref.shape[0]
    m = nb * SEQ_TILE
    n_blk = m // DN_CHUNK
    t_idx = pl.program_id(1)

    @pl.when(t_idx == 0)
    def _():
        sp_ref[...] = jnp.zeros_like(sp_ref)
        pre_ref[:, 0:HALO, :] = jnp.zeros((nb, HALO, QKV_DIM), jnp.float32)

    x = x_ref[...].reshape(m, D_MODEL)
    xn = x * lax.rsqrt(jnp.mean(x * x, axis=-1, keepdims=True) + EPS) * nrm_ref[...]
    xn = _bf16(xn)

    pre_ref[:, HALO:HALO + SEQ_TILE, :] = _dot(xn, w_ref[:, 0:QKV_DIM]).reshape(nb, SEQ_TILE, QKV_DIM)
    z_ref[...] = _dot(xn, w_ref[:, P_Z:P_BA])

    ba = _dot(xn, w_ref[:, P_BA:P_GM])
    g = -jnp.exp(alog_ref[...]) * _softplus(ba + dtb_ref[...])
    pos = _iota((m, LANES), 0) & (DN_CHUNK - 1)
    shift = 1
    while shift < DN_CHUNK:
        g = g + jnp.where(pos >= shift, pltpu.roll(g, shift, axis=0), 0.0)
        shift *= 2
    gates = jnp.where(_iota((m, LANES), 1) < DN_HEADS, _sigmoid(ba), g)

    for j in range(2 * GM_GROUPS):
        gm = _gelu_tanh(_dot(xn, w_ref[:, P_GM + j * LANES:P_GM + (j + 1) * LANES]))
        if j < GM_GROUPS:
            ug_ref[:, j * LANES:(j + 1) * LANES] = gm
        else:
            c0 = (j - GM_GROUPS) * LANES
            mu = jnp.mean(gm, axis=-1, keepdims=True)
            d = gm - mu
            var = jnp.mean(d * d, axis=-1, keepdims=True)
            vln_ref[:, c0:c0 + LANES] = (d * lax.rsqrt(var + EPS) * lng_ref[:, c0:c0 + LANES]
                                         + lnb_ref[:, c0:c0 + LANES])

    for j in range(QKV_DIM // LANES):
        c0 = j * LANES
        acc = pre_ref[:, HALO:HALO + SEQ_TILE, c0:c0 + LANES] * cw_ref[CONV_WIDTH - 1:CONV_WIDTH, c0:c0 + LANES]
        for s in range(1, CONV_WIDTH):
            acc = acc + (pre_ref[:, HALO - s:HALO - s + SEQ_TILE, c0:c0 + LANES]
                         * cw_ref[CONV_WIDTH - 1 - s:CONV_WIDTH - s, c0:c0 + LANES])
        y = _silu(acc).reshape(m, LANES)
        if j < 2 * DN_HEADS:
            y = y * lax.rsqrt(jnp.sum(y * y, axis=-1, keepdims=True) + EPS)
            if j < DN_HEADS:
                y = y * (DN_HEAD ** -0.5)
        qkv_ref[:, c0:c0 + LANES] = y
    pre_ref[:, 0:HALO, :] = pre_ref[:, SEQ_TILE:SEQ_TILE + HALO, :]

    pi = _iota((m, MAT_WIDTH), 0) & (DN_CHUNK - 1)
    pj = _iota((m, MAT_WIDTH), 1) & (DN_CHUNK - 1)
    incl = pi >= pj
    strict = pi > pj
    eye4 = jnp.where(pi == pj, 1.0, 0.0).astype(jnp.float32)
    mask_mat = (_iota((MAT_WIDTH, MAT_WIDTH), 0) >> CHUNK_SHIFT) == (_iota((MAT_WIDTH, MAT_WIDTH), 1) >> CHUNK_SHIFT)
    mask_head = (_iota((MAT_WIDTH, DN_WIDTH), 0) >> CHUNK_SHIFT) == (_iota((MAT_WIDTH, DN_WIDTH), 1) >> HEAD_SHIFT)
    mask_pair = (_iota((PAIR_WIDTH, PAIR_WIDTH), 0) >> HEAD_SHIFT) == (_iota((PAIR_WIDTH, PAIR_WIDTH), 1) >> HEAD_SHIFT)
    ones_blk = _bf16(jnp.where((_iota((m, m), 0) >> CHUNK_SHIFT) == (_iota((m, m), 1) >> CHUNK_SHIFT), 1.0, 0.0))
    si = _iota((LANES, 2 * MAT_WIDTH + DN_WIDTH), 0)
    sj = _iota((LANES, 2 * MAT_WIDTH + DN_WIDTH), 1)
    src = jnp.where(sj < MAT_WIDTH, sj >> CHUNK_SHIFT,
                    jnp.where(sj < 2 * MAT_WIDTH, DN_HEADS + ((sj - MAT_WIDTH) >> CHUNK_SHIFT),
                              DN_HEADS + ((sj - 2 * MAT_WIDTH) >> HEAD_SHIFT)))
    sel = _bf16(jnp.where(si == src, 1.0, 0.0))

    def blk(a, r):
        return a[r * DN_CHUNK:(r + 1) * DN_CHUNK]

    cols = _select_right(gates, sel)
    b_col = cols[:, 0:MAT_WIDTH]
    g_col = cols[:, MAT_WIDTH:2 * MAT_WIDTH]
    g_col_wide = cols[:, 2 * MAT_WIDTH:]
    rows = _select_left(ones_blk, jnp.concatenate([b_col * eye4, g_col * eye4], axis=1))
    b_row = rows[:, 0:MAT_WIDTH]
    g_row = rows[:, MAT_WIDTH:]
    decay = jnp.where(incl, jnp.exp(jnp.where(incl, g_col - g_row, 0.0)), 0.0)

    q = qkv_ref[:, 0:DN_WIDTH]
    k = qkv_ref[:, DN_WIDTH:2 * DN_WIDTH]
    q16 = _bf16(q)
    k16 = _bf16(k)
    v16 = _bf16(qkv_ref[:, 2 * DN_WIDTH:3 * DN_WIDTH])

    k_bd = [jnp.where(mask_head, _tile_rows(blk(k16, r), DN_HEADS), 0.0) for r in range(n_blk)]
    kq = [_dot_nt(jnp.concatenate([blk(k16, r), blk(q16, r)], axis=0), k_bd[r]) for r in range(n_blk)]
    kk = jnp.concatenate([t[0:DN_CHUNK] for t in kq], axis=0)
    qk = jnp.concatenate([t[DN_CHUNK:2 * DN_CHUNK] for t in kq], axis=0)
    a_mat = jnp.where(strict, kk * b_col * decay, 0.0)
    qk16 = _bf16(qk * decay)

    def mat_prod(lhs, rhs):
        l16 = _bf16(lhs)
        r16 = _bf16(rhs)
        return jnp.concatenate(
            [_dot(blk(l16, r), jnp.where(mask_mat, _tile_rows(blk(r16, r), DN_HEADS), 0.0))
             for r in range(n_blk)], axis=0)

    inv = eye4 - a_mat
    pw = a_mat
    for _ in range(5):
        pw = mat_prod(pw, pw)
        inv = inv + mat_prod(inv, pw)
    inv_b = inv * b_row
    inv_b16 = _bf16(inv_b)
    inv_bg16 = _bf16(inv_b * jnp.exp(g_row))
    u = jnp.concatenate(
        [_dot(blk(inv_b16, r), jnp.where(mask_head, _tile_rows(blk(v16, r), DN_HEADS), 0.0))
         for r in range(n_blk)], axis=0)
    w16 = _bf16(jnp.concatenate([_dot(blk(inv_bg16, r), k_bd[r]) for r in range(n_blk)], axis=0))
    qg16 = _bf16(q * jnp.exp(g_col_wide))
    g_last = jnp.concatenate(
        [jnp.broadcast_to(blk(g_col_wide, r)[DN_CHUNK - 1:DN_CHUNK, :], (DN_CHUNK, DN_WIDTH))
         for r in range(n_blk)], axis=0)
    kd16 = _bf16(k * jnp.exp(g_last - g_col_wide))
    s_decay = jnp.exp(g_last)

    o_blocks = [None] * n_blk
    for c in range(CHUNKS_PER_TILE):
        for b in range(nb):
            r = b * CHUNKS_PER_TILE + c
            s_old = [sp_ref[b * HEAD_PAIRS + p] for p in range(HEAD_PAIRS)]
            v_new = []
            o_state = []
            for p in range(HEAD_PAIRS):
                ls = slice(p * PAIR_WIDTH, (p + 1) * PAIR_WIDTH)
                lhs = jnp.concatenate([blk(w16, r)[:, ls], blk(qg16, r)[:, ls]], axis=0)
                wq = _dot(lhs, _bf16(s_old[p]))
                v_new.append(blk(u, r)[:, ls] - wq[0:DN_CHUNK])
                o_state.append(wq[DN_CHUNK:2 * DN_CHUNK])
            v_new16 = _bf16(jnp.concatenate(v_new, axis=1))
            o_blocks[r] = jnp.concatenate(o_state, axis=1) + _dot(
                blk(qk16, r), jnp.where(mask_head, _tile_rows(v_new16, DN_HEADS), 0.0))
            for p in range(HEAD_PAIRS):
                ls = slice(p * PAIR_WIDTH, (p + 1) * PAIR_WIDTH)
                upd = _dot_tn(blk(kd16, r)[:, ls], v_new16[:, ls])
                sp_ref[b * HEAD_PAIRS + p] = (s_old[p] * blk(s_decay, r)[0:1, ls]
                                              + jnp.where(mask_pair, upd, 0.0))
    o = jnp.concatenate(o_blocks, axis=0)

    for h in range(DN_HEADS):
        hs = slice(h * DN_HEAD, (h + 1) * DN_HEAD)
        oh = o[:, hs]
        oh = oh * lax.rsqrt(jnp.mean(oh * oh, axis=-1, keepdims=True) + EPS)
        o_ref[:, hs] = _bf16(oh * ong_ref[...] * _silu(z_ref[:, hs]))

    ti = _iota((GM_CHUNK, GM_CHUNK), 0)
    tj = _iota((GM_CHUNK, GM_CHUNK), 1)
    for gi in range(GM_GROUPS):
        c0 = gi * GM_GROUP_DIM
        ws_causal = _bf16(jnp.where(ti >= tj, ws_ref[gi], 0.0))
        for b in range(nb):
            rs = slice(b * SEQ_TILE, (b + 1) * SEQ_TILE)
            sp = _dot(ws_causal, _bf16(vln_ref[rs, c0:c0 + GM_GROUP_DIM])) + bs_ref[gi]
            o_ref[rs, DN_WIDTH + c0:DN_WIDTH + c0 + GM_GROUP_DIM] = _bf16(
                ug_ref[rs, c0:c0 + GM_GROUP_DIM] * sp)

    y = _dot(o_ref[...], wo_ref[...])
    out_ref[...] = (x + y).reshape(nb, SEQ_TILE, D_MODEL)


def _mixer(h, nrm, w_pack, cw, alog, dtb, ong, lng, lnb, ws, bs_b, wo):
    batch, seq, _ = h.shape
    nb = BATCH_TILE
    m = nb * SEQ_TILE
    const2 = lambda bi, ti: (0, 0)
    const3 = lambda bi, ti: (0, 0, 0)
    return pl.pallas_call(
        _mixer_kernel,
        out_shape=jax.ShapeDtypeStruct(h.shape, h.dtype),
        grid=(batch // nb, seq // SEQ_TILE),
        in_specs=[
            pl.BlockSpec((nb, SEQ_TILE, D_MODEL), lambda bi, ti: (bi, ti, 0)),
            pl.BlockSpec((1, D_MODEL), const2),
            pl.BlockSpec((D_MODEL, P_DIM), const2),
            pl.BlockSpec((CONV_WIDTH, QKV_DIM), const2),
            pl.BlockSpec((1, LANES), const2),
            pl.BlockSpec((1, LANES), const2),
            pl.BlockSpec((1, DN_HEAD), const2),
            pl.BlockSpec((1, GM_WIDTH), const2),
            pl.BlockSpec((1, GM_WIDTH), const2),
            pl.BlockSpec((GM_GROUPS, GM_CHUNK, GM_CHUNK), const3),
            pl.BlockSpec((GM_GROUPS, GM_CHUNK, GM_GROUP_DIM), const3),
            pl.BlockSpec((D_MODEL, D_MODEL), const2),
        ],
        out_specs=pl.BlockSpec((nb, SEQ_TILE, D_MODEL), lambda bi, ti: (bi, ti, 0)),
        scratch_shapes=[
            pltpu.VMEM((nb, HALO + SEQ_TILE, QKV_DIM), jnp.float32),
            pltpu.VMEM((m, QKV_DIM), jnp.float32),
            pltpu.VMEM((m, DN_WIDTH), jnp.float32),
            pltpu.VMEM((m, GM_WIDTH), jnp.float32),
            pltpu.VMEM((m, GM_WIDTH), jnp.float32),
            pltpu.VMEM((m, D_MODEL), jnp.bfloat16),
            pltpu.VMEM((nb * HEAD_PAIRS, PAIR_WIDTH, PAIR_WIDTH), jnp.float32),
        ],
        compiler_params=pltpu.CompilerParams(
            dimension_semantics=("arbitrary", "arbitrary"),
            vmem_limit_bytes=VMEM_LIMIT_BYTES),
        name="mixer",
    )(h, nrm, w_pack, cw, alog, dtb, ong, lng, lnb, ws, bs_b, wo)


def _ffn_kernel(x_ref, nrm_ref, wg_ref, wu_ref, wd_ref, fin_ref, out_ref, *, final_norm):
    x = x_ref[...]
    xn = _bf16(x * lax.rsqrt(jnp.mean(x * x, axis=-1, keepdims=True) + EPS) * nrm_ref[...])
    act = _bf16(_silu(_dot(xn, wg_ref[...])) * _dot(xn, wu_ref[...]))
    y = x + _dot(act, wd_ref[...])
    if final_norm:
        y = y * lax.rsqrt(jnp.mean(y * y, axis=-1, keepdims=True) + EPS) * fin_ref[...]
    out_ref[...] = y


def _ffn(h2, nrm, wg, wu, wd, fin, final_norm):
    n = h2.shape[0]
    const2 = lambda i: (0, 0)
    return pl.pallas_call(
        functools.partial(_ffn_kernel, final_norm=final_norm),
        out_shape=jax.ShapeDtypeStruct(h2.shape, h2.dtype),
        grid=(n // FFN_ROWS,),
        in_specs=[
            pl.BlockSpec((FFN_ROWS, D_MODEL), lambda i: (i, 0)),
            pl.BlockSpec((1, D_MODEL), const2),
            pl.BlockSpec((D_MODEL, D_FF), const2, pipeline_mode=pl.Buffered(1)),
            pl.BlockSpec((D_MODEL, D_FF), const2, pipeline_mode=pl.Buffered(1)),
            pl.BlockSpec((D_FF, D_MODEL), const2, pipeline_mode=pl.Buffered(1)),
            pl.BlockSpec((1, D_MODEL), const2),
        ],
        out_specs=pl.BlockSpec((FFN_ROWS, D_MODEL), lambda i: (i, 0)),
        compiler_params=pltpu.CompilerParams(
            dimension_semantics=("arbitrary",),
            vmem_limit_bytes=VMEM_LIMIT_BYTES),
        name="ffn",
    )(h2, nrm, wg, wu, wd, fin)


def _pad_lanes(v, offset):
    return jnp.zeros((1, LANES), jnp.float32).at[0, offset:offset + v.shape[0]].set(v)


def kernel(x, norm_mix, w_in, conv_w, a_log, dt_bias, o_norm_g, ln_v_g, ln_v_b, w_s, b_s, w_out,
           norm_ffn, w_gate, w_up, w_down, norm_final):
    batch, seq, d = x.shape
    depth = w_in.shape[0]
    h = x
    for l in range(depth):
        w_ba = jnp.pad(w_in[l][:, BETA_OFF:GM_OFF], ((0, 0), (0, LANES - 2 * DN_HEADS)))
        w_pack = _bf16(jnp.concatenate([w_in[l][:, :BETA_OFF], w_ba, w_in[l][:, GM_OFF:]], axis=1))
        bs_b = jnp.broadcast_to(b_s[l][:, :, None], (GM_GROUPS, GM_CHUNK, GM_GROUP_DIM))
        h = _mixer(h, norm_mix[l][None, :], w_pack, conv_w[l],
                   _pad_lanes(a_log[l], DN_HEADS), _pad_lanes(dt_bias[l], DN_HEADS),
                   o_norm_g[l][None, :], ln_v_g[l][None, :], ln_v_b[l][None, :],
                   w_s[l], bs_b, _bf16(w_out[l]))
        h = _ffn(h.reshape(batch * seq, d), norm_ffn[l][None, :], _bf16(w_gate[l]), _bf16(w_up[l]),
                 _bf16(w_down[l]), norm_final[None, :], l == depth - 1).reshape(batch, seq, d)
    return h
```

```python
import functools

import jax
import jax.numpy as jnp
from jax import lax
from jax.experimental import pallas as pl
from jax.experimental.pallas import tpu as pltpu

D_MODEL = 1024
DN_HEADS = 4
DN_HEAD = 128
DN_WIDTH = DN_HEADS * DN_HEAD
QKV_DIM = 3 * DN_WIDTH
CONV_WIDTH = 4
DN_CHUNK = 64
GM_GROUPS = 4
GM_GROUP_DIM = 128
GM_WIDTH = GM_GROUPS * GM_GROUP_DIM
GM_CHUNK = 128
D_FF = 2816
EPS = 1e-6

Z_OFF = QKV_DIM
BETA_OFF = Z_OFF + DN_WIDTH
GM_OFF = BETA_OFF + 2 * DN_HEADS
IN_DIM = GM_OFF + 2 * GM_WIDTH

LANES = 128
SUBLANES = 8

P_Z = QKV_DIM
P_BA = P_Z + DN_WIDTH
P_GM = P_BA + LANES
P_DIM = P_GM + 2 * GM_WIDTH

SEQ_TILE = GM_CHUNK
BATCH_TILE = 4
HALO = SUBLANES
FFN_ROWS = 512
VMEM_LIMIT_BYTES = 56 * 1024 * 1024

CHUNKS_PER_TILE = SEQ_TILE // DN_CHUNK
HEAD_PAIRS = DN_HEADS // 2
PAIR_WIDTH = 2 * DN_HEAD
MAT_WIDTH = DN_HEADS * DN_CHUNK
CHUNK_SHIFT = 6
HEAD_SHIFT = 7


def _dot(a, b):
    return jnp.dot(a, b, preferred_element_type=jnp.float32)


def _dot_nt(a, b):
    return lax.dot_general(a, b, (((1,), (1,)), ((), ())), preferred_element_type=jnp.float32)


def _dot_tn(a, b):
    return lax.dot_general(a, b, (((0,), (0,)), ((), ())), preferred_element_type=jnp.float32)


def _sigmoid(x):
    return 1.0 / (1.0 + jnp.exp(-x))


def _silu(x):
    return x * _sigmoid(x)


def _gelu_tanh(x):
    return 0.5 * x * (1.0 + jnp.tanh(0.7978845608028654 * (x + 0.044715 * (x * x * x))))


def _softplus(x):
    return jnp.maximum(x, 0.0) + jnp.log(1.0 + jnp.exp(-jnp.abs(x)))


def _bf16(x):
    return x.astype(jnp.bfloat16)


def _iota(shape, dim):
    return lax.broadcasted_iota(jnp.int32, shape, dim)


def _tile_rows(y, n):
    return jnp.concatenate([y] * n, axis=0)


def _mixer_kernel(x_ref, nrm_ref, w_ref, cw_ref, alog_ref, dtb_ref, ong_ref, lng_ref, lnb_ref,
                  ws_ref, bs_ref, wo_ref, out_ref,
                  pre_ref, qkv_ref, z_ref, ug_ref, vln_ref, o_ref, sp_ref):
    nb = x_ref.shape[0]
    m = nb * SEQ_TILE
    n_blk = m // DN_CHUNK
    t_idx = pl.program_id(1)

    @pl.when(t_idx == 0)
    def _():
        sp_ref[...] = jnp.zeros_like(sp_ref)
        pre_ref[:, 0:HALO, :] = jnp.zeros((nb, HALO, QKV_DIM), jnp.float32)

    x = x_ref[...].reshape(m, D_MODEL)
    xn = x * lax.rsqrt(jnp.mean(x * x, axis=-1, keepdims=True) + EPS) * nrm_ref[...]
    xn = _bf16(xn)

    proj = _dot(xn, w_ref[...])
    pre_ref[:, HALO:HALO + SEQ_TILE, :] = proj[:, 0:QKV_DIM].reshape(nb, SEQ_TILE, QKV_DIM)
    z_ref[...] = proj[:, P_Z:P_BA]

    ba = proj[:, P_BA:P_GM]
    g = -jnp.exp(alog_ref[...]) * _softplus(ba + dtb_ref[...])
    pos = _iota((m, LANES), 0) & (DN_CHUNK - 1)
    shift = 1
    while shift < DN_CHUNK:
        g = g + jnp.where(pos >= shift, pltpu.roll(g, shift, axis=0), 0.0)
        shift *= 2
    gates = jnp.where(_iota((m, LANES), 1) < DN_HEADS, _sigmoid(ba), g)

    for j in range(2 * GM_GROUPS):
        gm = _gelu_tanh(proj[:, P_GM + j * LANES:P_GM + (j + 1) * LANES])
        if j < GM_GROUPS:
            ug_ref[:, j * LANES:(j + 1) * LANES] = gm
        else:
            c0 = (j - GM_GROUPS) * LANES
            mu = jnp.mean(gm, axis=-1, keepdims=True)
            d = gm - mu
            var = jnp.mean(d * d, axis=-1, keepdims=True)
            vln_ref[:, c0:c0 + LANES] = (d * lax.rsqrt(var + EPS) * lng_ref[:, c0:c0 + LANES]
                                         + lnb_ref[:, c0:c0 + LANES])

    for j in range(QKV_DIM // LANES):
        c0 = j * LANES
        acc = pre_ref[:, HALO:HALO + SEQ_TILE, c0:c0 + LANES] * cw_ref[CONV_WIDTH - 1:CONV_WIDTH, c0:c0 + LANES]
        for s in range(1, CONV_WIDTH):
            acc = acc + (pre_ref[:, HALO - s:HALO - s + SEQ_TILE, c0:c0 + LANES]
                         * cw_ref[CONV_WIDTH - 1 - s:CONV_WIDTH - s, c0:c0 + LANES])
        y = _silu(acc).reshape(m, LANES)
        if j < 2 * DN_HEADS:
            y = y * lax.rsqrt(jnp.sum(y * y, axis=-1, keepdims=True) + EPS)
            if j < DN_HEADS:
                y = y * (DN_HEAD ** -0.5)
        qkv_ref[:, c0:c0 + LANES] = y
    pre_ref[:, 0:HALO, :] = pre_ref[:, SEQ_TILE:SEQ_TILE + HALO, :]

    pi = _iota((m, MAT_WIDTH), 0) & (DN_CHUNK - 1)
    pj = _iota((m, MAT_WIDTH), 1) & (DN_CHUNK - 1)
    incl = pi >= pj
    strict = pi > pj
    eye4 = jnp.where(pi == pj, 1.0, 0.0).astype(jnp.float32)
    mask_mat = (_iota((MAT_WIDTH, MAT_WIDTH), 0) >> CHUNK_SHIFT) == (_iota((MAT_WIDTH, MAT_WIDTH), 1) >> CHUNK_SHIFT)
    mask_head = (_iota((MAT_WIDTH, DN_WIDTH), 0) >> CHUNK_SHIFT) == (_iota((MAT_WIDTH, DN_WIDTH), 1) >> HEAD_SHIFT)
    mask_pair = (_iota((PAIR_WIDTH, PAIR_WIDTH), 0) >> HEAD_SHIFT) == (_iota((PAIR_WIDTH, PAIR_WIDTH), 1) >> HEAD_SHIFT)

    def blk(a, r):
        return a[r * DN_CHUNK:(r + 1) * DN_CHUNK]

    bc = [jnp.broadcast_to(gates[:, c:c + 1], (m, LANES)) for c in range(2 * DN_HEADS)]
    low_half = _iota((m, LANES), 1) < DN_CHUNK

    def pair_cols(base):
        return jnp.concatenate([jnp.where(low_half, bc[base + 2 * v], bc[base + 2 * v + 1])
                                for v in range(HEAD_PAIRS)], axis=1)

    b_col = pair_cols(0)
    g_col = pair_cols(DN_HEADS)
    g_col_wide = jnp.concatenate(bc[DN_HEADS:2 * DN_HEADS], axis=1)

    low_half_row = _iota((1, LANES), 1) < DN_CHUNK
    b_row = []
    g_row = []
    for b in range(nb):
        gt = gates[b * SEQ_TILE:(b + 1) * SEQ_TILE, :].T[0:2 * DN_HEADS, :]
        gt_swapped = pltpu.roll(gt, DN_CHUNK, axis=1)
        for c in range(CHUNKS_PER_TILE):
            lo_src, hi_src = (gt, gt_swapped) if c == 0 else (gt_swapped, gt)

            def pair_rows(base):
                vec = jnp.concatenate(
                    [jnp.where(low_half_row, lo_src[base + 2 * v:base + 2 * v + 1, :],
                               hi_src[base + 2 * v + 1:base + 2 * v + 2, :])
                     for v in range(HEAD_PAIRS)], axis=1)
                return jnp.broadcast_to(vec, (DN_CHUNK, MAT_WIDTH))

            b_row.append(pair_rows(0))
            g_row.append(pair_rows(DN_HEADS))
    b_row = jnp.concatenate(b_row, axis=0)
    g_row = jnp.concatenate(g_row, axis=0)
    decay = jnp.where(incl, jnp.exp(jnp.where(incl, g_col - g_row, 0.0)), 0.0)

    q = qkv_ref[:, 0:DN_WIDTH]
    k = qkv_ref[:, DN_WIDTH:2 * DN_WIDTH]
    q16 = _bf16(q)
    k16 = _bf16(k)
    v16 = _bf16(qkv_ref[:, 2 * DN_WIDTH:3 * DN_WIDTH])

    k_bd = [jnp.where(mask_head, _tile_rows(blk(k16, r), DN_HEADS), 0.0) for r in range(n_blk)]
    kq = [_dot_nt(jnp.concatenate([blk(k16, r), blk(q16, r)], axis=0), k_bd[r]) for r in range(n_blk)]
    kk = jnp.concatenate([t[0:DN_CHUNK] for t in kq], axis=0)
    qk = jnp.concatenate([t[DN_CHUNK:2 * DN_CHUNK] for t in kq], axis=0)
    a_mat = jnp.where(strict, kk * b_col * decay, 0.0)
    qk16 = _bf16(qk * decay)

    def mat_prod(lhs, rhs):
        l16 = _bf16(lhs)
        r16 = _bf16(rhs)
        return jnp.concatenate(
            [_dot(blk(l16, r), jnp.where(mask_mat, _tile_rows(blk(r16, r), DN_HEADS), 0.0))
             for r in range(n_blk)], axis=0)

    inv = eye4 - a_mat
    pw = a_mat
    for _ in range(5):
        pw = mat_prod(pw, pw)
        inv = inv + mat_prod(inv, pw)
    inv_b = inv * b_row
    inv_b16 = _bf16(inv_b)
    inv_bg16 = _bf16(inv_b * jnp.exp(g_row))
    u = jnp.concatenate(
        [_dot(blk(inv_b16, r), jnp.where(mask_head, _tile_rows(blk(v16, r), DN_HEADS), 0.0))
         for r in range(n_blk)], axis=0)
    w16 = _bf16(jnp.concatenate([_dot(blk(inv_bg16, r), k_bd[r]) for r in range(n_blk)], axis=0))
    qg16 = _bf16(q * jnp.exp(g_col_wide))
    g_last = jnp.concatenate(
        [jnp.broadcast_to(blk(g_col_wide, r)[DN_CHUNK - 1:DN_CHUNK, :], (DN_CHUNK, DN_WIDTH))
         for r in range(n_blk)], axis=0)
    kd16 = _bf16(k * jnp.exp(g_last - g_col_wide))
    s_decay = jnp.exp(g_last)

    o_blocks = [None] * n_blk
    for c in range(CHUNKS_PER_TILE):
        for b in range(nb):
            r = b * CHUNKS_PER_TILE + c
            s_old = [sp_ref[b * HEAD_PAIRS + p] for p in range(HEAD_PAIRS)]
            v_new = []
            o_state = []
            for p in range(HEAD_PAIRS):
                ls = slice(p * PAIR_WIDTH, (p + 1) * PAIR_WIDTH)
                lhs = jnp.concatenate([blk(w16, r)[:, ls], blk(qg16, r)[:, ls]], axis=0)
                wq = _dot(lhs, _bf16(s_old[p]))
                v_new.append(blk(u, r)[:, ls] - wq[0:DN_CHUNK])
                o_state.append(wq[DN_CHUNK:2 * DN_CHUNK])
            v_new16 = _bf16(jnp.concatenate(v_new, axis=1))
            o_blocks[r] = jnp.concatenate(o_state, axis=1) + _dot(
                blk(qk16, r), jnp.where(mask_head, _tile_rows(v_new16, DN_HEADS), 0.0))
            for p in range(HEAD_PAIRS):
                ls = slice(p * PAIR_WIDTH, (p + 1) * PAIR_WIDTH)
                upd = _dot_tn(blk(kd16, r)[:, ls], v_new16[:, ls])
                sp_ref[b * HEAD_PAIRS + p] = (s_old[p] * blk(s_decay, r)[0:1, ls]
                                              + jnp.where(mask_pair, upd, 0.0))
    o = jnp.concatenate(o_blocks, axis=0)

    for h in range(DN_HEADS):
        hs = slice(h * DN_HEAD, (h + 1) * DN_HEAD)
        oh = o[:, hs]
        oh = oh * lax.rsqrt(jnp.mean(oh * oh, axis=-1, keepdims=True) + EPS)
        o_ref[:, hs] = _bf16(oh * ong_ref[...] * _silu(z_ref[:, hs]))

    ti = _iota((GM_CHUNK, GM_CHUNK), 0)
    tj = _iota((GM_CHUNK, GM_CHUNK), 1)
    for gi in range(GM_GROUPS):
        c0 = gi * GM_GROUP_DIM
        ws_causal = _bf16(jnp.where(ti >= tj, ws_ref[gi], 0.0))
        for b in range(nb):
            rs = slice(b * SEQ_TILE, (b + 1) * SEQ_TILE)
            sp = _dot(ws_causal, _bf16(vln_ref[rs, c0:c0 + GM_GROUP_DIM])) + bs_ref[gi]
            o_ref[rs, DN_WIDTH + c0:DN_WIDTH + c0 + GM_GROUP_DIM] = _bf16(
                ug_ref[rs, c0:c0 + GM_GROUP_DIM] * sp)

    y = _dot(o_ref[...], wo_ref[...])
    out_ref[...] = (x + y).reshape(nb, SEQ_TILE, D_MODEL)


def _layer_spec(shape, layer, n_grid):
    zeros = (0,) * len(shape)
    if n_grid == 2:
        return pl.BlockSpec((None,) + shape, lambda bi, ti: (layer,) + zeros)
    return pl.BlockSpec((None,) + shape, lambda i: (layer,) + zeros)


def _mixer(h, layer, nrm, w_pack, cw, alog, dtb, ong, lng, lnb, ws, bs_b, wo):
    batch, seq, _ = h.shape
    nb = BATCH_TILE
    m = nb * SEQ_TILE
    spec = functools.partial(_layer_spec, layer=layer, n_grid=2)
    return pl.pallas_call(
        _mixer_kernel,
        out_shape=jax.ShapeDtypeStruct(h.shape, h.dtype),
        grid=(batch // nb, seq // SEQ_TILE),
        in_specs=[
            pl.BlockSpec((nb, SEQ_TILE, D_MODEL), lambda bi, ti: (bi, ti, 0)),
            spec((1, D_MODEL)),
            spec((D_MODEL, P_DIM)),
            spec((CONV_WIDTH, QKV_DIM)),
            spec((1, LANES)),
            spec((1, LANES)),
            spec((1, DN_HEAD)),
            spec((1, GM_WIDTH)),
            spec((1, GM_WIDTH)),
            spec((GM_GROUPS, GM_CHUNK, GM_CHUNK)),
            spec((GM_GROUPS, GM_CHUNK, GM_GROUP_DIM)),
            spec((D_MODEL, D_MODEL)),
        ],
        out_specs=pl.BlockSpec((nb, SEQ_TILE, D_MODEL), lambda bi, ti: (bi, ti, 0)),
        scratch_shapes=[
            pltpu.VMEM((nb, HALO + SEQ_TILE, QKV_DIM), jnp.float32),
            pltpu.VMEM((m, QKV_DIM), jnp.float32),
            pltpu.VMEM((m, DN_WIDTH), jnp.float32),
            pltpu.VMEM((m, GM_WIDTH), jnp.float32),
            pltpu.VMEM((m, GM_WIDTH), jnp.float32),
            pltpu.VMEM((m, D_MODEL), jnp.bfloat16),
            pltpu.VMEM((nb * HEAD_PAIRS, PAIR_WIDTH, PAIR_WIDTH), jnp.float32),
        ],
        compiler_params=pltpu.CompilerParams(
            dimension_semantics=("arbitrary", "arbitrary"),
            vmem_limit_bytes=VMEM_LIMIT_BYTES),
        name="mixer",
    )(h, nrm, w_pack, cw, alog, dtb, ong, lng, lnb, ws, bs_b, wo)


def _ffn_kernel(x_ref, nrm_ref, wg_ref, wu_ref, wd_ref, fin_ref, out_ref, *, final_norm):
    x = x_ref[...]
    xn = _bf16(x * lax.rsqrt(jnp.mean(x * x, axis=-1, keepdims=True) + EPS) * nrm_ref[...])
    act = _bf16(_silu(_dot(xn, wg_ref[...])) * _dot(xn, wu_ref[...]))
    y = x + _dot(act, wd_ref[...])
    if final_norm:
        y = y * lax.rsqrt(jnp.mean(y * y, axis=-1, keepdims=True) + EPS) * fin_ref[...]
    out_ref[...] = y


def _ffn(h2, layer, nrm, wg, wu, wd, fin, final_norm):
    n = h2.shape[0]

    def resident(shape):
        return pl.BlockSpec((None,) + shape, lambda i: (layer, 0, 0), pipeline_mode=pl.Buffered(1))

    return pl.pallas_call(
        functools.partial(_ffn_kernel, final_norm=final_norm),
        out_shape=jax.ShapeDtypeStruct(h2.shape, h2.dtype),
        grid=(n // FFN_ROWS,),
        in_specs=[
            pl.BlockSpec((FFN_ROWS, D_MODEL), lambda i: (i, 0)),
            _layer_spec((1, D_MODEL), layer, 1),
            resident((D_MODEL, D_FF)),
            resident((D_MODEL, D_FF)),
            resident((D_FF, D_MODEL)),
            pl.BlockSpec((1, D_MODEL), lambda i: (0, 0)),
        ],
        out_specs=pl.BlockSpec((FFN_ROWS, D_MODEL), lambda i: (i, 0)),
        compiler_params=pltpu.CompilerParams(
            dimension_semantics=("arbitrary",),
            vmem_limit_bytes=VMEM_LIMIT_BYTES),
        name="ffn",
    )(h2, nrm, wg, wu, wd, fin)


def _pad_lanes(v, offset):
    return jnp.pad(v, ((0, 0), (offset, LANES - offset - v.shape[1])))[:, None, :]


def kernel(x, norm_mix, w_in, conv_w, a_log, dt_bias, o_norm_g, ln_v_g, ln_v_b, w_s, b_s, w_out,
           norm_ffn, w_gate, w_up, w_down, norm_final):
    batch, seq, d = x.shape
    depth = w_in.shape[0]
    w_ba = jnp.pad(w_in[:, :, BETA_OFF:GM_OFF], ((0, 0), (0, 0), (0, LANES - 2 * DN_HEADS)))
    w_pack = _bf16(jnp.concatenate([w_in[:, :, :BETA_OFF], w_ba, w_in[:, :, GM_OFF:]], axis=2))
    wo16, wg16, wu16, wd16 = _bf16(w_out), _bf16(w_gate), _bf16(w_up), _bf16(w_down)
    bs_b = jnp.broadcast_to(b_s[:, :, :, None], (depth, GM_GROUPS, GM_CHUNK, GM_GROUP_DIM))
    alog_p = _pad_lanes(a_log, DN_HEADS)
    dtb_p = _pad_lanes(dt_bias, DN_HEADS)
    row = lambda a: a[:, None, :]
    h = x
    for l in range(depth):
        h = _mixer(h, l, row(norm_mix), w_pack, conv_w, alog_p, dtb_p, row(o_norm_g), row(ln_v_g),
                   row(ln_v_b), w_s, bs_b, wo16)
        h = _ffn(h.reshape(batch * seq, d), l, row(norm_ffn), wg16, wu16, wd16, norm_final[None, :],
                 l == depth - 1).reshape(batch, seq, d)
    return h
```

```python
import functools

import jax
import jax.numpy as jnp
from jax import lax
from jax.experimental import pallas as pl
from jax.experimental.pallas import tpu as pltpu

D_MODEL = 1024
DN_HEADS = 4
DN_HEAD = 128
DN_WIDTH = DN_HEADS * DN_HEAD
QKV_DIM = 3 * DN_WIDTH
CONV_WIDTH = 4
DN_CHUNK = 64
GM_GROUPS = 4
GM_GROUP_DIM = 128
GM_WIDTH = GM_GROUPS * GM_GROUP_DIM
GM_CHUNK = 128
D_FF = 2816
EPS = 1e-6

Z_OFF = QKV_DIM
BETA_OFF = Z_OFF + DN_WIDTH
GM_OFF = BETA_OFF + 2 * DN_HEADS
IN_DIM = GM_OFF + 2 * GM_WIDTH

LANES = 128
SUBLANES = 8

P_GM = QKV_DIM
P_BA = P_GM + 2 * GM_WIDTH
P_Z = P_BA + LANES
P_DIM = P_Z + DN_WIDTH

SEQ_TILE = GM_CHUNK
BATCH_TILE = 4
HALO = SUBLANES
QKV_TILES = QKV_DIM // LANES
FFN_ROWS = 512
VMEM_LIMIT_BYTES = 56 * 1024 * 1024

CHUNKS_PER_TILE = SEQ_TILE // DN_CHUNK
HEAD_PAIRS = DN_HEADS // 2
PAIR_WIDTH = 2 * DN_HEAD
MAT_WIDTH = DN_HEADS * DN_CHUNK
CHUNK_SHIFT = 6
HEAD_SHIFT = 7

GELU_C = 0.7978845608028654


def _dot(a, b):
    return jnp.dot(a, b, preferred_element_type=jnp.float32)


def _dot_nt(a, b):
    return lax.dot_general(a, b, (((1,), (1,)), ((), ())), preferred_element_type=jnp.float32)


def _dot_tn(a, b):
    return lax.dot_general(a, b, (((0,), (0,)), ((), ())), preferred_element_type=jnp.float32)


def _sigmoid(x):
    return 1.0 / (1.0 + jnp.exp(-x))


def _silu(x):
    return x * _sigmoid(x)


def _gelu_tanh(x):
    t = jnp.tanh(x * (GELU_C + (0.044715 * GELU_C) * (x * x)))
    return (0.5 * x) * (1.0 + t)


def _softplus(x):
    return jnp.maximum(x, 0.0) + jnp.log(1.0 + jnp.exp(-jnp.abs(x)))


def _bf16(x):
    return x.astype(jnp.bfloat16)


def _iota(shape, dim):
    return lax.broadcasted_iota(jnp.int32, shape, dim)


def _tile_rows(y, n):
    return jnp.concatenate([y] * n, axis=0)


def _mixer_kernel(x_ref, nrm_ref, w_ref, cw_ref, alog_ref, dtb_ref, ong_ref, lng_ref, lnb_ref,
                  ws_ref, bs_ref, wo_ref, out_ref,
                  pre_ref, qkv_ref, z_ref, ug_ref, vln_ref, o_ref, sp_ref):
    nb = x_ref.shape[0]
    m = nb * SEQ_TILE
    n_blk = m // DN_CHUNK
    t_idx = pl.program_id(1)

    @pl.when(t_idx == 0)
    def _():
        sp_ref[...] = jnp.zeros_like(sp_ref)
        pre_ref[:, :, 0:HALO, :] = jnp.zeros((QKV_TILES, nb, HALO, LANES), jnp.float32)

    x = x_ref[...].reshape(m, D_MODEL)
    xn = x * lax.rsqrt(jnp.mean(x * x, axis=-1, keepdims=True) + EPS) * nrm_ref[...]
    xn = _bf16(xn)

    proj = _dot(xn, w_ref[...])
    z_ref[...] = proj[:, P_Z:P_DIM]

    for j in range(QKV_TILES):
        c0 = j * LANES
        cur = proj[:, c0:c0 + LANES].reshape(nb, SEQ_TILE, LANES)
        pre_ref[j, :, HALO:HALO + SEQ_TILE, :] = cur
        acc = cur * cw_ref[CONV_WIDTH - 1:CONV_WIDTH, c0:c0 + LANES]
        for s in range(1, CONV_WIDTH):
            acc = acc + (pre_ref[j, :, HALO - s:HALO - s + SEQ_TILE, :]
                         * cw_ref[CONV_WIDTH - 1 - s:CONV_WIDTH - s, c0:c0 + LANES])
        y = _silu(acc).reshape(m, LANES)
        if j < 2 * DN_HEADS:
            y = y * lax.rsqrt(jnp.sum(y * y, axis=-1, keepdims=True) + EPS)
            if j < DN_HEADS:
                y = y * (DN_HEAD ** -0.5)
        qkv_ref[:, c0:c0 + LANES] = y
    pre_ref[:, :, 0:HALO, :] = pre_ref[:, :, SEQ_TILE:SEQ_TILE + HALO, :]

    for j in range(2 * GM_GROUPS):
        gm = _gelu_tanh(proj[:, P_GM + j * LANES:P_GM + (j + 1) * LANES])
        if j < GM_GROUPS:
            ug_ref[:, j * LANES:(j + 1) * LANES] = gm
        else:
            c0 = (j - GM_GROUPS) * LANES
            mu = jnp.mean(gm, axis=-1, keepdims=True)
            d = gm - mu
            var = jnp.mean(d * d, axis=-1, keepdims=True)
            vln_ref[:, c0:c0 + LANES] = (d * lax.rsqrt(var + EPS) * lng_ref[:, c0:c0 + LANES]
                                         + lnb_ref[:, c0:c0 + LANES])

    ba = proj[:, P_BA:P_Z]
    g = -jnp.exp(alog_ref[...]) * _softplus(ba + dtb_ref[...])
    pos = _iota((m, LANES), 0) & (DN_CHUNK - 1)
    shift = 1
    while shift < DN_CHUNK:
        g = g + jnp.where(pos >= shift, pltpu.roll(g, shift, axis=0), 0.0)
        shift *= 2
    gates = jnp.where(_iota((m, LANES), 1) < DN_HEADS, _sigmoid(ba), g)

    pi = _iota((m, MAT_WIDTH), 0) & (DN_CHUNK - 1)
    pj = _iota((m, MAT_WIDTH), 1) & (DN_CHUNK - 1)
    incl = pi >= pj
    strict = pi > pj
    eye4 = jnp.where(pi == pj, 1.0, 0.0).astype(jnp.float32)
    mask_mat = (_iota((MAT_WIDTH, MAT_WIDTH), 0) >> CHUNK_SHIFT) == (_iota((MAT_WIDTH, MAT_WIDTH), 1) >> CHUNK_SHIFT)
    mask_head = (_iota((MAT_WIDTH, DN_WIDTH), 0) >> CHUNK_SHIFT) == (_iota((MAT_WIDTH, DN_WIDTH), 1) >> HEAD_SHIFT)
    mask_pair = (_iota((PAIR_WIDTH, PAIR_WIDTH), 0) >> HEAD_SHIFT) == (_iota((PAIR_WIDTH, PAIR_WIDTH), 1) >> HEAD_SHIFT)

    def blk(a, r):
        return a[r * DN_CHUNK:(r + 1) * DN_CHUNK]

    bc = [jnp.broadcast_to(gates[:, c:c + 1], (m, LANES)) for c in range(2 * DN_HEADS)]
    low_half = _iota((m, LANES), 1) < DN_CHUNK

    def pair_cols(base):
        return jnp.concatenate([jnp.where(low_half, bc[base + 2 * v], bc[base + 2 * v + 1])
                                for v in range(HEAD_PAIRS)], axis=1)

    b_col = pair_cols(0)
    g_col = pair_cols(DN_HEADS)
    g_col_wide = jnp.concatenate(bc[DN_HEADS:2 * DN_HEADS], axis=1)

    low_half_row = _iota((1, LANES), 1) < DN_CHUNK
    b_row = []
    g_row = []
    for b in range(nb):
        gt = gates[b * SEQ_TILE:(b + 1) * SEQ_TILE, :].T[0:2 * DN_HEADS, :]
        gt_swapped = pltpu.roll(gt, DN_CHUNK, axis=1)
        for c in range(CHUNKS_PER_TILE):
            lo_src, hi_src = (gt, gt_swapped) if c == 0 else (gt_swapped, gt)

            def pair_rows(base):
                vec = jnp.concatenate(
                    [jnp.where(low_half_row, lo_src[base + 2 * v:base + 2 * v + 1, :],
                               hi_src[base + 2 * v + 1:base + 2 * v + 2, :])
                     for v in range(HEAD_PAIRS)], axis=1)
                return jnp.broadcast_to(vec, (DN_CHUNK, MAT_WIDTH))

            b_row.append(pair_rows(0))
            g_row.append(pair_rows(DN_HEADS))
    b_row = jnp.concatenate(b_row, axis=0)
    g_row = jnp.concatenate(g_row, axis=0)
    decay = jnp.where(incl, jnp.exp(jnp.where(incl, g_col - g_row, 0.0)), 0.0)

    q = qkv_ref[:, 0:DN_WIDTH]
    k = qkv_ref[:, DN_WIDTH:2 * DN_WIDTH]
    q16 = _bf16(q)
    k16 = _bf16(k)
    v16 = _bf16(qkv_ref[:, 2 * DN_WIDTH:3 * DN_WIDTH])

    k_bd = [jnp.where(mask_head, _tile_rows(blk(k16, r), DN_HEADS), 0.0) for r in range(n_blk)]
    kq = [_dot_nt(jnp.concatenate([blk(k16, r), blk(q16, r)], axis=0), k_bd[r]) for r in range(n_blk)]
    kk = jnp.concatenate([t[0:DN_CHUNK] for t in kq], axis=0)
    qk = jnp.concatenate([t[DN_CHUNK:2 * DN_CHUNK] for t in kq], axis=0)
    a_mat = jnp.where(strict, kk * b_col * decay, 0.0)
    qk16 = _bf16(qk * decay)

    def mat_prod(lhs, rhs):
        l16 = _bf16(lhs)
        r16 = _bf16(rhs)
        return jnp.concatenate(
            [_dot(blk(l16, r), jnp.where(mask_mat, _tile_rows(blk(r16, r), DN_HEADS), 0.0))
             for r in range(n_blk)], axis=0)

    inv = eye4 - a_mat
    pw = a_mat
    for _ in range(5):
        pw = mat_prod(pw, pw)
        inv = inv + mat_prod(inv, pw)
    inv_b = inv * b_row
    inv_b16 = _bf16(inv_b)
    inv_bg16 = _bf16(inv_b * jnp.exp(g_row))
    u = jnp.concatenate(
        [_dot(blk(inv_b16, r), jnp.where(mask_head, _tile_rows(blk(v16, r), DN_HEADS), 0.0))
         for r in range(n_blk)], axis=0)
    w16 = _bf16(jnp.concatenate([_dot(blk(inv_bg16, r), k_bd[r]) for r in range(n_blk)], axis=0))
    qg16 = _bf16(q * jnp.exp(g_col_wide))
    g_last = jnp.concatenate(
        [jnp.broadcast_to(blk(g_col_wide, r)[DN_CHUNK - 1:DN_CHUNK, :], (DN_CHUNK, DN_WIDTH))
         for r in range(n_blk)], axis=0)
    kd16 = _bf16(k * jnp.exp(g_last - g_col_wide))
    s_decay = jnp.exp(g_last)

    o_blocks = [None] * n_blk
    for c in range(CHUNKS_PER_TILE):
        for b in range(nb):
            r = b * CHUNKS_PER_TILE + c
            s_old = [sp_ref[b * HEAD_PAIRS + p] for p in range(HEAD_PAIRS)]
            v_new = []
            o_state = []
            for p in range(HEAD_PAIRS):
                ls = slice(p * PAIR_WIDTH, (p + 1) * PAIR_WIDTH)
                lhs = jnp.concatenate([blk(w16, r)[:, ls], blk(qg16, r)[:, ls]], axis=0)
                wq = _dot(lhs, _bf16(s_old[p]))
                v_new.append(blk(u, r)[:, ls] - wq[0:DN_CHUNK])
                o_state.append(wq[DN_CHUNK:2 * DN_CHUNK])
            v_new16 = _bf16(jnp.concatenate(v_new, axis=1))
            o_blocks[r] = jnp.concatenate(o_state, axis=1) + _dot(
                blk(qk16, r), jnp.where(mask_head, _tile_rows(v_new16, DN_HEADS), 0.0))
            for p in range(HEAD_PAIRS):
                ls = slice(p * PAIR_WIDTH, (p + 1) * PAIR_WIDTH)
                upd = _dot_tn(blk(kd16, r)[:, ls], v_new16[:, ls])
                sp_ref[b * HEAD_PAIRS + p] = (s_old[p] * blk(s_decay, r)[0:1, ls]
                                              + jnp.where(mask_pair, upd, 0.0))
    o = jnp.concatenate(o_blocks, axis=0)

    for h in range(DN_HEADS):
        hs = slice(h * DN_HEAD, (h + 1) * DN_HEAD)
        oh = o[:, hs]
        oh = oh * lax.rsqrt(jnp.mean(oh * oh, axis=-1, keepdims=True) + EPS)
        o_ref[:, hs] = _bf16(oh * ong_ref[...] * _silu(z_ref[:, hs]))

    ti = _iota((GM_CHUNK, GM_CHUNK), 0)
    tj = _iota((GM_CHUNK, GM_CHUNK), 1)
    for gi in range(GM_GROUPS):
        c0 = gi * GM_GROUP_DIM
        ws_causal = _bf16(jnp.where(ti >= tj, ws_ref[gi], 0.0))
        for b in range(nb):
            rs = slice(b * SEQ_TILE, (b + 1) * SEQ_TILE)
            sp = _dot(ws_causal, _bf16(vln_ref[rs, c0:c0 + GM_GROUP_DIM])) + bs_ref[gi]
            o_ref[rs, DN_WIDTH + c0:DN_WIDTH + c0 + GM_GROUP_DIM] = _bf16(
                ug_ref[rs, c0:c0 + GM_GROUP_DIM] * sp)

    y = _dot(o_ref[...], wo_ref[...])
    out_ref[...] = (x + y).reshape(nb, SEQ_TILE, D_MODEL)


def _layer_spec(shape, layer, n_grid):
    zeros = (0,) * len(shape)
    if n_grid == 2:
        return pl.BlockSpec((None,) + shape, lambda bi, ti: (layer,) + zeros)
    return pl.BlockSpec((None,) + shape, lambda i: (layer,) + zeros)


def _mixer(h, layer, nrm, w_pack, cw, alog, dtb, ong, lng, lnb, ws, bs_b, wo):
    batch, seq, _ = h.shape
    nb = BATCH_TILE
    m = nb * SEQ_TILE
    spec = functools.partial(_layer_spec, layer=layer, n_grid=2)
    return pl.pallas_call(
        _mixer_kernel,
        out_shape=jax.ShapeDtypeStruct(h.shape, h.dtype),
        grid=(batch // nb, seq // SEQ_TILE),
        in_specs=[
            pl.BlockSpec((nb, SEQ_TILE, D_MODEL), lambda bi, ti: (bi, ti, 0)),
            spec((1, D_MODEL)),
            spec((D_MODEL, P_DIM)),
            spec((CONV_WIDTH, QKV_DIM)),
            spec((1, LANES)),
            spec((1, LANES)),
            spec((1, DN_HEAD)),
            spec((1, GM_WIDTH)),
            spec((1, GM_WIDTH)),
            spec((GM_GROUPS, GM_CHUNK, GM_CHUNK)),
            spec((GM_GROUPS, GM_CHUNK, GM_GROUP_DIM)),
            spec((D_MODEL, D_MODEL)),
        ],
        out_specs=pl.BlockSpec((nb, SEQ_TILE, D_MODEL), lambda bi, ti: (bi, ti, 0)),
        scratch_shapes=[
            pltpu.VMEM((QKV_TILES, nb, HALO + SEQ_TILE, LANES), jnp.float32),
            pltpu.VMEM((m, QKV_DIM), jnp.float32),
            pltpu.VMEM((m, DN_WIDTH), jnp.float32),
            pltpu.VMEM((m, GM_WIDTH), jnp.float32),
            pltpu.VMEM((m, GM_WIDTH), jnp.float32),
            pltpu.VMEM((m, D_MODEL), jnp.bfloat16),
            pltpu.VMEM((nb * HEAD_PAIRS, PAIR_WIDTH, PAIR_WIDTH), jnp.float32),
        ],
        compiler_params=pltpu.CompilerParams(
            dimension_semantics=("arbitrary", "arbitrary"),
            vmem_limit_bytes=VMEM_LIMIT_BYTES),
        name="mixer",
    )(h, nrm, w_pack, cw, alog, dtb, ong, lng, lnb, ws, bs_b, wo)


def _ffn_kernel(x_ref, nrm_ref, wg_ref, wu_ref, wd_ref, fin_ref, out_ref, *, final_norm):
    x = x_ref[...]
    xn = _bf16(x * lax.rsqrt(jnp.mean(x * x, axis=-1, keepdims=True) + EPS) * nrm_ref[...])
    act = _bf16(_silu(_dot(xn, wg_ref[...])) * _dot(xn, wu_ref[...]))
    y = x + _dot(act, wd_ref[...])
    if final_norm:
        y = y * lax.rsqrt(jnp.mean(y * y, axis=-1, keepdims=True) + EPS) * fin_ref[...]
    out_ref[...] = y


def _ffn(h2, layer, nrm, wg, wu, wd, fin, final_norm):
    n = h2.shape[0]

    def resident(shape):
        return pl.BlockSpec((None,) + shape, lambda i: (layer, 0, 0), pipeline_mode=pl.Buffered(1))

    return pl.pallas_call(
        functools.partial(_ffn_kernel, final_norm=final_norm),
        out_shape=jax.ShapeDtypeStruct(h2.shape, h2.dtype),
        grid=(n // FFN_ROWS,),
        in_specs=[
            pl.BlockSpec((FFN_ROWS, D_MODEL), lambda i: (i, 0)),
            _layer_spec((1, D_MODEL), layer, 1),
            resident((D_MODEL, D_FF)),
            resident((D_MODEL, D_FF)),
            resident((D_FF, D_MODEL)),
            pl.BlockSpec((1, D_MODEL), lambda i: (0, 0)),
        ],
        out_specs=pl.BlockSpec((FFN_ROWS, D_MODEL), lambda i: (i, 0)),
        compiler_params=pltpu.CompilerParams(
            dimension_semantics=("arbitrary",),
            vmem_limit_bytes=VMEM_LIMIT_BYTES),
        name="ffn",
    )(h2, nrm, wg, wu, wd, fin)


def _pad_lanes(v, offset):
    return jnp.pad(v, ((0, 0), (offset, LANES - offset - v.shape[1])))[:, None, :]


def kernel(x, norm_mix, w_in, conv_w, a_log, dt_bias, o_norm_g, ln_v_g, ln_v_b, w_s, b_s, w_out,
           norm_ffn, w_gate, w_up, w_down, norm_final):
    batch, seq, d = x.shape
    depth = w_in.shape[0]
    w_ba = jnp.pad(w_in[:, :, BETA_OFF:GM_OFF], ((0, 0), (0, 0), (0, LANES - 2 * DN_HEADS)))
    w_pack = _bf16(jnp.concatenate(
        [w_in[:, :, :Z_OFF], w_in[:, :, GM_OFF:], w_ba, w_in[:, :, Z_OFF:BETA_OFF]], axis=2))
    wo16, wg16, wu16, wd16 = _bf16(w_out), _bf16(w_gate), _bf16(w_up), _bf16(w_down)
    bs_b = jnp.broadcast_to(b_s[:, :, :, None], (depth, GM_GROUPS, GM_CHUNK, GM_GROUP_DIM))
    alog_p = _pad_lanes(a_log, DN_HEADS)
    dtb_p = _pad_lanes(dt_bias, DN_HEADS)
    row = lambda a: a[:, None, :]
    h = x
    for l in range(depth):
        h = _mixer(h, l, row(norm_mix), w_pack, conv_w, alog_p, dtb_p, row(o_norm_g), row(ln_v_g),
                   row(ln_v_b), w_s, bs_b, wo16)
        h = _ffn(h.reshape(batch * seq, d), l, row(norm_ffn), wg16, wu16, wd16, norm_final[None, :],
                 l == depth - 1).reshape(batch, seq, d)
    return h
```

```python
import functools

import jax
import jax.numpy as jnp
from jax import lax
from jax.experimental import pallas as pl
from jax.experimental.pallas import tpu as pltpu

D_MODEL = 1024
DN_HEADS = 4
DN_HEAD = 128
DN_WIDTH = DN_HEADS * DN_HEAD
QKV_DIM = 3 * DN_WIDTH
CONV_WIDTH = 4
DN_CHUNK = 64
GM_GROUPS = 4
GM_GROUP_DIM = 128
GM_WIDTH = GM_GROUPS * GM_GROUP_DIM
GM_CHUNK = 128
D_FF = 2816
EPS = 1e-6

Z_OFF = QKV_DIM
BETA_OFF = Z_OFF + DN_WIDTH
GM_OFF = BETA_OFF + 2 * DN_HEADS
IN_DIM = GM_OFF + 2 * GM_WIDTH

LANES = 128
SUBLANES = 8

P_BA = QKV_DIM
P_GM = P_BA + LANES
P_Z = P_GM + 2 * GM_WIDTH
P_DIM = P_Z + DN_WIDTH

SEQ_TILE = GM_CHUNK
BATCH_TILE = 4
HALO = SUBLANES
QKV_TILES = QKV_DIM // LANES
FFN_ROWS = 512
VMEM_LIMIT_BYTES = 56 * 1024 * 1024

CHUNKS_PER_TILE = SEQ_TILE // DN_CHUNK
HEAD_PAIRS = DN_HEADS // 2
PAIR_WIDTH = 2 * DN_HEAD
MAT_WIDTH = DN_HEADS * DN_CHUNK
CHUNK_SHIFT = 6
HEAD_SHIFT = 7

GELU_C = 0.7978845608028654


def _dot(a, b):
    return jnp.dot(a, b, preferred_element_type=jnp.float32)


def _dot_nt(a, b):
    return lax.dot_general(a, b, (((1,), (1,)), ((), ())), preferred_element_type=jnp.float32)


def _dot_tn(a, b):
    return lax.dot_general(a, b, (((0,), (0,)), ((), ())), preferred_element_type=jnp.float32)


def _sigmoid(x):
    return 1.0 / (1.0 + jnp.exp(-x))


def _silu(x):
    return x * _sigmoid(x)


def _gelu_tanh(x):
    t = jnp.tanh(x * (GELU_C + (0.044715 * GELU_C) * (x * x)))
    return (0.5 * x) * (1.0 + t)


def _softplus(x):
    return jnp.maximum(x, 0.0) + jnp.log(1.0 + jnp.exp(-jnp.abs(x)))


def _bf16(x):
    return x.astype(jnp.bfloat16)


def _iota(shape, dim):
    return lax.broadcasted_iota(jnp.int32, shape, dim)


def _tile_rows(y, n):
    return jnp.concatenate([y] * n, axis=0)


def _mixer_kernel(x_ref, nrm_ref, w_ref, cw_ref, alog_ref, dtb_ref, ong_ref, lng_ref, lnb_ref,
                  ws_ref, bs_ref, wo_ref, out_ref,
                  pre_ref, qkv_ref, z_ref, ug_ref, vln_ref, sp_ref):
    nb = x_ref.shape[0]
    m = nb * SEQ_TILE
    n_blk = m // DN_CHUNK
    t_idx = pl.program_id(1)

    @pl.when(t_idx == 0)
    def _():
        sp_ref[...] = jnp.zeros_like(sp_ref)
        pre_ref[:, :, 0:HALO, :] = jnp.zeros((QKV_TILES, nb, HALO, LANES), jnp.float32)

    x = x_ref[...].reshape(m, D_MODEL)
    xn = x * lax.rsqrt(jnp.mean(x * x, axis=-1, keepdims=True) + EPS) * nrm_ref[...]
    xn = _bf16(xn)

    proj = _dot(xn, w_ref[...])
    z_ref[...] = proj[:, P_Z:P_DIM]

    for j in range(QKV_TILES):
        c0 = j * LANES
        cur = proj[:, c0:c0 + LANES].reshape(nb, SEQ_TILE, LANES)
        pre_ref[j, :, HALO:HALO + SEQ_TILE, :] = cur
        acc = cur * cw_ref[CONV_WIDTH - 1:CONV_WIDTH, c0:c0 + LANES]
        for s in range(1, CONV_WIDTH):
            acc = acc + (pre_ref[j, :, HALO - s:HALO - s + SEQ_TILE, :]
                         * cw_ref[CONV_WIDTH - 1 - s:CONV_WIDTH - s, c0:c0 + LANES])
        y = _silu(acc).reshape(m, LANES)
        if j < 2 * DN_HEADS:
            y = y * lax.rsqrt(jnp.sum(y * y, axis=-1, keepdims=True) + EPS)
            if j < DN_HEADS:
                y = y * (DN_HEAD ** -0.5)
        qkv_ref[:, c0:c0 + LANES] = y
    pre_ref[:, :, 0:HALO, :] = pre_ref[:, :, SEQ_TILE:SEQ_TILE + HALO, :]

    for j in range(2 * GM_GROUPS):
        gm = _gelu_tanh(proj[:, P_GM + j * LANES:P_GM + (j + 1) * LANES])
        if j < GM_GROUPS:
            ug_ref[:, j * LANES:(j + 1) * LANES] = gm
        else:
            c0 = (j - GM_GROUPS) * LANES
            mu = jnp.mean(gm, axis=-1, keepdims=True)
            d = gm - mu
            var = jnp.mean(d * d, axis=-1, keepdims=True)
            vln_ref[:, c0:c0 + LANES] = (d * lax.rsqrt(var + EPS) * lng_ref[:, c0:c0 + LANES]
                                         + lnb_ref[:, c0:c0 + LANES])

    ba = proj[:, P_BA:P_BA + LANES]
    g = -jnp.exp(alog_ref[...]) * _softplus(ba + dtb_ref[...])
    pos = _iota((m, LANES), 0) & (DN_CHUNK - 1)
    shift = 1
    while shift < DN_CHUNK:
        g = g + jnp.where(pos >= shift, pltpu.roll(g, shift, axis=0), 0.0)
        shift *= 2
    gates = jnp.where(_iota((m, LANES), 1) < DN_HEADS, _sigmoid(ba), g)

    pi = _iota((m, MAT_WIDTH), 0) & (DN_CHUNK - 1)
    pj = _iota((m, MAT_WIDTH), 1) & (DN_CHUNK - 1)
    incl = pi >= pj
    strict = pi > pj
    eye4 = jnp.where(pi == pj, 1.0, 0.0).astype(jnp.float32)
    mask_mat = (_iota((MAT_WIDTH, MAT_WIDTH), 0) >> CHUNK_SHIFT) == (_iota((MAT_WIDTH, MAT_WIDTH), 1) >> CHUNK_SHIFT)
    mask_head = (_iota((MAT_WIDTH, DN_WIDTH), 0) >> CHUNK_SHIFT) == (_iota((MAT_WIDTH, DN_WIDTH), 1) >> HEAD_SHIFT)
    mask_pair = (_iota((PAIR_WIDTH, PAIR_WIDTH), 0) >> HEAD_SHIFT) == (_iota((PAIR_WIDTH, PAIR_WIDTH), 1) >> HEAD_SHIFT)

    def blk(a, r):
        return a[r * DN_CHUNK:(r + 1) * DN_CHUNK]

    bc = [jnp.broadcast_to(gates[:, c:c + 1], (m, LANES)) for c in range(2 * DN_HEADS)]
    low_half = _iota((m, LANES), 1) < DN_CHUNK

    def pair_cols(base):
        return jnp.concatenate([jnp.where(low_half, bc[base + 2 * v], bc[base + 2 * v + 1])
                                for v in range(HEAD_PAIRS)], axis=1)

    b_col = pair_cols(0)
    g_col = pair_cols(DN_HEADS)
    g_col_wide = jnp.concatenate(bc[DN_HEADS:2 * DN_HEADS], axis=1)

    low_half_row = _iota((1, LANES), 1) < DN_CHUNK
    b_row = []
    g_row = []
    for b in range(nb):
        gt = gates[b * SEQ_TILE:(b + 1) * SEQ_TILE, :].T[0:2 * DN_HEADS, :]
        gt_swapped = pltpu.roll(gt, DN_CHUNK, axis=1)
        for c in range(CHUNKS_PER_TILE):
            lo_src, hi_src = (gt, gt_swapped) if c == 0 else (gt_swapped, gt)

            def pair_rows(base):
                vec = jnp.concatenate(
                    [jnp.where(low_half_row, lo_src[base + 2 * v:base + 2 * v + 1, :],
                               hi_src[base + 2 * v + 1:base + 2 * v + 2, :])
                     for v in range(HEAD_PAIRS)], axis=1)
                return jnp.broadcast_to(vec, (DN_CHUNK, MAT_WIDTH))

            b_row.append(pair_rows(0))
            g_row.append(pair_rows(DN_HEADS))
    b_row = jnp.concatenate(b_row, axis=0)
    g_row = jnp.concatenate(g_row, axis=0)
    decay = jnp.where(incl, jnp.exp(jnp.where(incl, g_col - g_row, 0.0)), 0.0)

    q = qkv_ref[:, 0:DN_WIDTH]
    k = qkv_ref[:, DN_WIDTH:2 * DN_WIDTH]
    q16 = _bf16(q)
    k16 = _bf16(k)
    v16 = _bf16(qkv_ref[:, 2 * DN_WIDTH:3 * DN_WIDTH])

    k_bd = [jnp.where(mask_head, _tile_rows(blk(k16, r), DN_HEADS), 0.0) for r in range(n_blk)]
    kq = [_dot_nt(jnp.concatenate([blk(k16, r), blk(q16, r)], axis=0), k_bd[r]) for r in range(n_blk)]
    kk = jnp.concatenate([t[0:DN_CHUNK] for t in kq], axis=0)
    qk = jnp.concatenate([t[DN_CHUNK:2 * DN_CHUNK] for t in kq], axis=0)
    a_mat = jnp.where(strict, kk * b_col * decay, 0.0)
    qk16 = _bf16(qk * decay)

    def times(lhs16, x16):
        outs = [[] for _ in lhs16]
        for r in range(n_blk):
            rhs = jnp.where(mask_mat, _tile_rows(blk(x16, r), DN_HEADS), 0.0)
            res = _dot(jnp.concatenate([blk(l, r) for l in lhs16], axis=0), rhs)
            for i in range(len(lhs16)):
                outs[i].append(res[i * DN_CHUNK:(i + 1) * DN_CHUNK])
        return [jnp.concatenate(o, axis=0) for o in outs]

    pw = -a_mat
    inv = eye4 + pw
    pw16 = _bf16(pw)
    (pw,) = times([pw16], pw16)
    for _ in range(4):
        pw16 = _bf16(pw)
        pw, step = times([pw16, _bf16(inv)], pw16)
        inv = inv + step
    (step,) = times([_bf16(inv)], _bf16(pw))
    inv = inv + step
    inv_b = inv * b_row
    inv_b16 = _bf16(inv_b)
    inv_bg16 = _bf16(inv_b * jnp.exp(g_row))
    u = jnp.concatenate(
        [_dot(blk(inv_b16, r), jnp.where(mask_head, _tile_rows(blk(v16, r), DN_HEADS), 0.0))
         for r in range(n_blk)], axis=0)
    w16 = _bf16(jnp.concatenate([_dot(blk(inv_bg16, r), k_bd[r]) for r in range(n_blk)], axis=0))
    qg16 = _bf16(q * jnp.exp(g_col_wide))
    g_last = jnp.concatenate(
        [jnp.broadcast_to(blk(g_col_wide, r)[DN_CHUNK - 1:DN_CHUNK, :], (DN_CHUNK, DN_WIDTH))
         for r in range(n_blk)], axis=0)
    kd16 = _bf16(k * jnp.exp(g_last - g_col_wide))
    s_decay = jnp.exp(g_last)

    o_blocks = [None] * n_blk
    for c in range(CHUNKS_PER_TILE):
        for b in range(nb):
            r = b * CHUNKS_PER_TILE + c
            s_old = [sp_ref[b * HEAD_PAIRS + p] for p in range(HEAD_PAIRS)]
            v_new = []
            o_state = []
            for p in range(HEAD_PAIRS):
                ls = slice(p * PAIR_WIDTH, (p + 1) * PAIR_WIDTH)
                lhs = jnp.concatenate([blk(w16, r)[:, ls], blk(qg16, r)[:, ls]], axis=0)
                wq = _dot(lhs, _bf16(s_old[p]))
                v_new.append(blk(u, r)[:, ls] - wq[0:DN_CHUNK])
                o_state.append(wq[DN_CHUNK:2 * DN_CHUNK])
            v_new16 = _bf16(jnp.concatenate(v_new, axis=1))
            o_blocks[r] = jnp.concatenate(o_state, axis=1) + _dot(
                blk(qk16, r), jnp.where(mask_head, _tile_rows(v_new16, DN_HEADS), 0.0))
            for p in range(HEAD_PAIRS):
                ls = slice(p * PAIR_WIDTH, (p + 1) * PAIR_WIDTH)
                upd = _dot_tn(blk(kd16, r)[:, ls], v_new16[:, ls])
                sp_ref[b * HEAD_PAIRS + p] = (s_old[p] * blk(s_decay, r)[0:1, ls]
                                              + jnp.where(mask_pair, upd, 0.0))
    o = jnp.concatenate(o_blocks, axis=0)

    ti = _iota((GM_CHUNK, GM_CHUNK), 0)
    tj = _iota((GM_CHUNK, GM_CHUNK), 1)
    y = None
    for p in range(GM_GROUPS // 2):
        pair = []
        for gi in range(2 * p, 2 * p + 2):
            c0 = gi * GM_GROUP_DIM
            ws_causal = _bf16(jnp.where(ti >= tj, ws_ref[gi], 0.0))
            rows = []
            for b in range(nb):
                rs = slice(b * SEQ_TILE, (b + 1) * SEQ_TILE)
                sp = _dot(ws_causal, _bf16(vln_ref[rs, c0:c0 + GM_GROUP_DIM])) + bs_ref[gi]
                rows.append(_bf16(ug_ref[rs, c0:c0 + GM_GROUP_DIM] * sp))
            pair.append(jnp.concatenate(rows, axis=0))
        part = _dot(jnp.concatenate(pair, axis=1),
                    wo_ref[DN_WIDTH + p * PAIR_WIDTH:DN_WIDTH + (p + 1) * PAIR_WIDTH, :])
        y = part if y is None else y + part

    for p in range(HEAD_PAIRS):
        pair = []
        for h in range(2 * p, 2 * p + 2):
            hs = slice(h * DN_HEAD, (h + 1) * DN_HEAD)
            oh = o[:, hs]
            oh = oh * lax.rsqrt(jnp.mean(oh * oh, axis=-1, keepdims=True) + EPS)
            pair.append(_bf16(oh * ong_ref[...] * _silu(z_ref[:, hs])))
        y = y + _dot(jnp.concatenate(pair, axis=1), wo_ref[p * PAIR_WIDTH:(p + 1) * PAIR_WIDTH, :])
    out_ref[...] = (x + y).reshape(nb, SEQ_TILE, D_MODEL)


def _layer_spec(shape, layer, n_grid):
    zeros = (0,) * len(shape)
    if n_grid == 2:
        return pl.BlockSpec((None,) + shape, lambda bi, ti: (layer,) + zeros)
    return pl.BlockSpec((None,) + shape, lambda i: (layer,) + zeros)


def _mixer(h, layer, nrm, w_pack, cw, alog, dtb, ong, lng, lnb, ws, bs_b, wo):
    batch, seq, _ = h.shape
    nb = BATCH_TILE
    m = nb * SEQ_TILE
    spec = functools.partial(_layer_spec, layer=layer, n_grid=2)
    return pl.pallas_call(
        _mixer_kernel,
        out_shape=jax.ShapeDtypeStruct(h.shape, h.dtype),
        grid=(batch // nb, seq // SEQ_TILE),
        in_specs=[
            pl.BlockSpec((nb, SEQ_TILE, D_MODEL), lambda bi, ti: (bi, ti, 0)),
            spec((1, D_MODEL)),
            spec((D_MODEL, P_DIM)),
            spec((CONV_WIDTH, QKV_DIM)),
            spec((1, LANES)),
            spec((1, LANES)),
            spec((1, DN_HEAD)),
            spec((1, GM_WIDTH)),
            spec((1, GM_WIDTH)),
            spec((GM_GROUPS, GM_CHUNK, GM_CHUNK)),
            spec((GM_GROUPS, GM_CHUNK, GM_GROUP_DIM)),
            spec((D_MODEL, D_MODEL)),
        ],
        out_specs=pl.BlockSpec((nb, SEQ_TILE, D_MODEL), lambda bi, ti: (bi, ti, 0)),
        scratch_shapes=[
            pltpu.VMEM((QKV_TILES, nb, HALO + SEQ_TILE, LANES), jnp.float32),
            pltpu.VMEM((m, QKV_DIM), jnp.float32),
            pltpu.VMEM((m, DN_WIDTH), jnp.float32),
            pltpu.VMEM((m, GM_WIDTH), jnp.float32),
            pltpu.VMEM((m, GM_WIDTH), jnp.float32),
            pltpu.VMEM((nb * HEAD_PAIRS, PAIR_WIDTH, PAIR_WIDTH), jnp.float32),
        ],
        compiler_params=pltpu.CompilerParams(
            dimension_semantics=("arbitrary", "arbitrary"),
            vmem_limit_bytes=VMEM_LIMIT_BYTES),
        name="mixer",
    )(h, nrm, w_pack, cw, alog, dtb, ong, lng, lnb, ws, bs_b, wo)


def _ffn_kernel(x_ref, nrm_ref, wg_ref, wu_ref, wd_ref, fin_ref, out_ref, *, final_norm):
    x = x_ref[...]
    xn = _bf16(x * lax.rsqrt(jnp.mean(x * x, axis=-1, keepdims=True) + EPS) * nrm_ref[...])
    act = _bf16(_silu(_dot(xn, wg_ref[...])) * _dot(xn, wu_ref[...]))
    y = x + _dot(act, wd_ref[...])
    if final_norm:
        y = y * lax.rsqrt(jnp.mean(y * y, axis=-1, keepdims=True) + EPS) * fin_ref[...]
    out_ref[...] = y


def _ffn(h2, layer, nrm, wg, wu, wd, fin, final_norm):
    n = h2.shape[0]

    def resident(shape):
        return pl.BlockSpec((None,) + shape, lambda i: (layer, 0, 0), pipeline_mode=pl.Buffered(1))

    return pl.pallas_call(
        functools.partial(_ffn_kernel, final_norm=final_norm),
        out_shape=jax.ShapeDtypeStruct(h2.shape, h2.dtype),
        grid=(n // FFN_ROWS,),
        in_specs=[
            pl.BlockSpec((FFN_ROWS, D_MODEL), lambda i: (i, 0)),
            _layer_spec((1, D_MODEL), layer, 1),
            resident((D_MODEL, D_FF)),
            resident((D_MODEL, D_FF)),
            resident((D_FF, D_MODEL)),
            pl.BlockSpec((1, D_MODEL), lambda i: (0, 0)),
        ],
        out_specs=pl.BlockSpec((FFN_ROWS, D_MODEL), lambda i: (i, 0)),
        compiler_params=pltpu.CompilerParams(
            dimension_semantics=("arbitrary",),
            vmem_limit_bytes=VMEM_LIMIT_BYTES),
        name="ffn",
    )(h2, nrm, wg, wu, wd, fin)


def _pad_lanes(v, offset):
    return jnp.pad(v, ((0, 0), (offset, LANES - offset - v.shape[1])))[:, None, :]


def kernel(x, norm_mix, w_in, conv_w, a_log, dt_bias, o_norm_g, ln_v_g, ln_v_b, w_s, b_s, w_out,
           norm_ffn, w_gate, w_up, w_down, norm_final):
    batch, seq, d = x.shape
    depth = w_in.shape[0]
    w_ba = jnp.pad(w_in[:, :, BETA_OFF:GM_OFF], ((0, 0), (0, 0), (0, LANES - 2 * DN_HEADS)))
    w_pack = _bf16(jnp.concatenate(
        [w_in[:, :, :Z_OFF], w_ba, w_in[:, :, GM_OFF:], w_in[:, :, Z_OFF:BETA_OFF]], axis=2))
    wo16, wg16, wu16, wd16 = _bf16(w_out), _bf16(w_gate), _bf16(w_up), _bf16(w_down)
    bs_b = jnp.broadcast_to(b_s[:, :, :, None], (depth, GM_GROUPS, GM_CHUNK, GM_GROUP_DIM))
    alog_p = _pad_lanes(a_log, DN_HEADS)
    dtb_p = _pad_lanes(dt_bias, DN_HEADS)
    row = lambda a: a[:, None, :]
    h = x
    for l in range(depth):
        h = _mixer(h, l, row(norm_mix), w_pack, conv_w, alog_p, dtb_p, row(o_norm_g), row(ln_v_g),
                   row(ln_v_b), w_s, bs_b, wo16)
        h = _ffn(h.reshape(batch * seq, d), l, row(norm_ffn), wg16, wu16, wd16, norm_final[None, :],
                 l == depth - 1).reshape(batch, seq, d)
    return h
```

```python
import functools

import jax
import jax.numpy as jnp
from jax import lax
from jax.experimental import pallas as pl
from jax.experimental.pallas import tpu as pltpu

D_MODEL = 1024
DN_HEADS = 4
DN_HEAD = 128
DN_WIDTH = DN_HEADS * DN_HEAD
QKV_DIM = 3 * DN_WIDTH
CONV_WIDTH = 4
DN_CHUNK = 64
GM_GROUPS = 4
GM_GROUP_DIM = 128
GM_WIDTH = GM_GROUPS * GM_GROUP_DIM
GM_CHUNK = 128
D_FF = 2816
EPS = 1e-6

Z_OFF = QKV_DIM
BETA_OFF = Z_OFF + DN_WIDTH
GM_OFF = BETA_OFF + 2 * DN_HEADS
IN_DIM = GM_OFF + 2 * GM_WIDTH

LANES = 128
SUBLANES = 8
BF16_SUBLANES = 16

P_BA = QKV_DIM
P_GM = P_BA + LANES
P_Z = P_GM + 2 * GM_WIDTH
P_DIM = P_Z + DN_WIDTH

SEQ_TILE = GM_CHUNK
BATCH_TILE = 4
HALO = SUBLANES
QKV_TILES = QKV_DIM // LANES
FFN_ROWS = 512
VMEM_LIMIT_BYTES = 56 * 1024 * 1024

CHUNKS_PER_TILE = SEQ_TILE // DN_CHUNK
HEAD_PAIRS = DN_HEADS // 2
PAIR_WIDTH = 2 * DN_HEAD
MAT_WIDTH = DN_HEADS * DN_CHUNK
CHUNK_SHIFT = 6
HEAD_SHIFT = 7

GELU_C = 0.7978845608028654


def _dot(a, b):
    return jnp.dot(a, b, preferred_element_type=jnp.float32)


def _dot_nt(a, b):
    return lax.dot_general(a, b, (((1,), (1,)), ((), ())), preferred_element_type=jnp.float32)


def _dot_tn(a, b):
    return lax.dot_general(a, b, (((0,), (0,)), ((), ())), preferred_element_type=jnp.float32)


def _sigmoid(x):
    return 1.0 / (1.0 + jnp.exp(-x))


def _silu(x):
    return x * _sigmoid(x)


def _gelu_tanh(x):
    t = jnp.tanh(x * (GELU_C + (0.044715 * GELU_C) * (x * x)))
    return (0.5 * x) * (1.0 + t)


def _softplus(x):
    return jnp.maximum(x, 0.0) + jnp.log(1.0 + jnp.exp(-jnp.abs(x)))


def _bf16(x):
    return x.astype(jnp.bfloat16)


def _iota(shape, dim):
    return lax.broadcasted_iota(jnp.int32, shape, dim)


def _tile_rows(y, n):
    return jnp.concatenate([y] * n, axis=0)


def _mixer_kernel(x_ref, nrm_ref, w_ref, cw_ref, alog_ref, dtb_ref, ong_ref, lng_ref, lnb_ref,
                  ws_ref, bs_ref, wo_ref, wg_f32_ref, wu_f32_ref, wd_f32_ref,
                  out_ref, wg16_ref, wu16_ref, wd16_ref,
                  pre_ref, qkv_ref, z_ref, ug_ref, vln_ref, sp_ref):
    wg16_ref[...] = _bf16(wg_f32_ref[...])
    wu16_ref[...] = _bf16(wu_f32_ref[...])
    wd16_ref[...] = _bf16(wd_f32_ref[...])

    nb = x_ref.shape[0]
    m = nb * SEQ_TILE
    n_blk = m // DN_CHUNK
    t_idx = pl.program_id(1)

    @pl.when(t_idx == 0)
    def _():
        sp_ref[...] = jnp.zeros_like(sp_ref)
        pre_ref[:, :, 0:HALO, :] = jnp.zeros((QKV_TILES, nb, HALO, LANES), jnp.float32)

    x = x_ref[...].reshape(m, D_MODEL)
    xn = x * lax.rsqrt(jnp.mean(x * x, axis=-1, keepdims=True) + EPS) * nrm_ref[...]
    xn = _bf16(xn)

    proj = _dot(xn, w_ref[...])
    z_ref[...] = proj[:, P_Z:P_DIM]

    for j in range(QKV_TILES):
        c0 = j * LANES
        cur = proj[:, c0:c0 + LANES].reshape(nb, SEQ_TILE, LANES)
        pre_ref[j, :, HALO:HALO + SEQ_TILE, :] = cur
        acc = cur * cw_ref[CONV_WIDTH - 1:CONV_WIDTH, c0:c0 + LANES]
        for s in range(1, CONV_WIDTH):
            acc = acc + (pre_ref[j, :, HALO - s:HALO - s + SEQ_TILE, :]
                         * cw_ref[CONV_WIDTH - 1 - s:CONV_WIDTH - s, c0:c0 + LANES])
        y = _silu(acc).reshape(m, LANES)
        if j < 2 * DN_HEADS:
            y = y * lax.rsqrt(jnp.sum(y * y, axis=-1, keepdims=True) + EPS)
            if j < DN_HEADS:
                y = y * (DN_HEAD ** -0.5)
        qkv_ref[:, c0:c0 + LANES] = y
    pre_ref[:, :, 0:HALO, :] = pre_ref[:, :, SEQ_TILE:SEQ_TILE + HALO, :]

    for j in range(2 * GM_GROUPS):
        gm = _gelu_tanh(proj[:, P_GM + j * LANES:P_GM + (j + 1) * LANES])
        if j < GM_GROUPS:
            ug_ref[:, j * LANES:(j + 1) * LANES] = gm
        else:
            c0 = (j - GM_GROUPS) * LANES
            mu = jnp.mean(gm, axis=-1, keepdims=True)
            d = gm - mu
            var = jnp.mean(d * d, axis=-1, keepdims=True)
            vln_ref[:, c0:c0 + LANES] = (d * lax.rsqrt(var + EPS) * lng_ref[:, c0:c0 + LANES]
                                         + lnb_ref[:, c0:c0 + LANES])

    ba = proj[:, P_BA:P_BA + LANES]
    g = -jnp.exp(alog_ref[...]) * _softplus(ba + dtb_ref[...])
    pos = _iota((m, LANES), 0) & (DN_CHUNK - 1)
    shift = 1
    while shift < DN_CHUNK:
        g = g + jnp.where(pos >= shift, pltpu.roll(g, shift, axis=0), 0.0)
        shift *= 2
    gates = jnp.where(_iota((m, LANES), 1) < DN_HEADS, _sigmoid(ba), g)

    pi = _iota((m, MAT_WIDTH), 0) & (DN_CHUNK - 1)
    pj = _iota((m, MAT_WIDTH), 1) & (DN_CHUNK - 1)
    incl = pi >= pj
    strict = pi > pj
    eye4 = jnp.where(pi == pj, 1.0, 0.0).astype(jnp.float32)
    mask_mat = (_iota((MAT_WIDTH, MAT_WIDTH), 0) >> CHUNK_SHIFT) == (_iota((MAT_WIDTH, MAT_WIDTH), 1) >> CHUNK_SHIFT)
    mask_head = (_iota((MAT_WIDTH, DN_WIDTH), 0) >> CHUNK_SHIFT) == (_iota((MAT_WIDTH, DN_WIDTH), 1) >> HEAD_SHIFT)
    mask_pair = (_iota((PAIR_WIDTH, PAIR_WIDTH), 0) >> HEAD_SHIFT) == (_iota((PAIR_WIDTH, PAIR_WIDTH), 1) >> HEAD_SHIFT)

    def blk(a, r):
        return a[r * DN_CHUNK:(r + 1) * DN_CHUNK]

    bc = [jnp.broadcast_to(gates[:, c:c + 1], (m, LANES)) for c in range(2 * DN_HEADS)]
    low_half = _iota((m, LANES), 1) < DN_CHUNK

    def pair_cols(base):
        return jnp.concatenate([jnp.where(low_half, bc[base + 2 * v], bc[base + 2 * v + 1])
                                for v in range(HEAD_PAIRS)], axis=1)

    b_col = pair_cols(0)
    g_col = pair_cols(DN_HEADS)
    g_col_wide = jnp.concatenate(bc[DN_HEADS:2 * DN_HEADS], axis=1)

    low_half_row = _iota((1, LANES), 1) < DN_CHUNK
    b_row = []
    g_row = []
    for b in range(nb):
        gt = gates[b * SEQ_TILE:(b + 1) * SEQ_TILE, :].T[0:2 * DN_HEADS, :]
        gt_swapped = pltpu.roll(gt, DN_CHUNK, axis=1)
        for c in range(CHUNKS_PER_TILE):
            lo_src, hi_src = (gt, gt_swapped) if c == 0 else (gt_swapped, gt)

            def pair_rows(base):
                vec = jnp.concatenate(
                    [jnp.where(low_half_row, lo_src[base + 2 * v:base + 2 * v + 1, :],
                               hi_src[base + 2 * v + 1:base + 2 * v + 2, :])
                     for v in range(HEAD_PAIRS)], axis=1)
                return jnp.broadcast_to(vec, (DN_CHUNK, MAT_WIDTH))

            b_row.append(pair_rows(0))
            g_row.append(pair_rows(DN_HEADS))
    b_row = jnp.concatenate(b_row, axis=0)
    g_row = jnp.concatenate(g_row, axis=0)
    decay = jnp.where(incl, jnp.exp(jnp.where(incl, g_col - g_row, 0.0)), 0.0)

    q = qkv_ref[:, 0:DN_WIDTH]
    k = qkv_ref[:, DN_WIDTH:2 * DN_WIDTH]
    q16 = _bf16(q)
    k16 = _bf16(k)
    v16 = _bf16(qkv_ref[:, 2 * DN_WIDTH:3 * DN_WIDTH])

    k_bd = [jnp.where(mask_head, _tile_rows(blk(k16, r), DN_HEADS), 0.0) for r in range(n_blk)]
    kq = [_dot_nt(jnp.concatenate([blk(k16, r), blk(q16, r)], axis=0), k_bd[r]) for r in range(n_blk)]
    kk = jnp.concatenate([t[0:DN_CHUNK] for t in kq], axis=0)
    qk = jnp.concatenate([t[DN_CHUNK:2 * DN_CHUNK] for t in kq], axis=0)
    a_mat = jnp.where(strict, kk * b_col * decay, 0.0)
    qk16 = _bf16(qk * decay)

    def times(lhs16, x16):
        outs = [[] for _ in lhs16]
        for r in range(n_blk):
            rhs = jnp.where(mask_mat, _tile_rows(blk(x16, r), DN_HEADS), 0.0)
            res = _dot(jnp.concatenate([blk(l, r) for l in lhs16], axis=0), rhs)
            for i in range(len(lhs16)):
                outs[i].append(res[i * DN_CHUNK:(i + 1) * DN_CHUNK])
        return [jnp.concatenate(o, axis=0) for o in outs]

    pw = -a_mat
    inv = eye4 + pw
    pw16 = _bf16(pw)
    (pw,) = times([pw16], pw16)
    for _ in range(4):
        pw16 = _bf16(pw)
        pw, step = times([pw16, _bf16(inv)], pw16)
        inv = inv + step
    (step,) = times([_bf16(inv)], _bf16(pw))
    inv = inv + step
    inv_b = inv * b_row
    inv_b16 = _bf16(inv_b)
    inv_bg16 = _bf16(inv_b * jnp.exp(g_row))
    u = jnp.concatenate(
        [_dot(blk(inv_b16, r), jnp.where(mask_head, _tile_rows(blk(v16, r), DN_HEADS), 0.0))
         for r in range(n_blk)], axis=0)
    w16 = _bf16(jnp.concatenate([_dot(blk(inv_bg16, r), k_bd[r]) for r in range(n_blk)], axis=0))
    qg16 = _bf16(q * jnp.exp(g_col_wide))
    g_last = jnp.concatenate(
        [jnp.broadcast_to(blk(g_col_wide, r)[DN_CHUNK - 1:DN_CHUNK, :], (DN_CHUNK, DN_WIDTH))
         for r in range(n_blk)], axis=0)
    kd16 = _bf16(k * jnp.exp(g_last - g_col_wide))
    s_decay = jnp.exp(g_last)

    o_blocks = [None] * n_blk
    for c in range(CHUNKS_PER_TILE):
        for b in range(nb):
            r = b * CHUNKS_PER_TILE + c
            s_old = [sp_ref[b * HEAD_PAIRS + p] for p in range(HEAD_PAIRS)]
            v_new = []
            o_state = []
            for p in range(HEAD_PAIRS):
                ls = slice(p * PAIR_WIDTH, (p + 1) * PAIR_WIDTH)
                lhs = jnp.concatenate([blk(w16, r)[:, ls], blk(qg16, r)[:, ls]], axis=0)
                wq = _dot(lhs, _bf16(s_old[p]))
                v_new.append(blk(u, r)[:, ls] - wq[0:DN_CHUNK])
                o_state.append(wq[DN_CHUNK:2 * DN_CHUNK])
            v_new16 = _bf16(jnp.concatenate(v_new, axis=1))
            o_blocks[r] = jnp.concatenate(o_state, axis=1) + _dot(
                blk(qk16, r), jnp.where(mask_head, _tile_rows(v_new16, DN_HEADS), 0.0))
            for p in range(HEAD_PAIRS):
                ls = slice(p * PAIR_WIDTH, (p + 1) * PAIR_WIDTH)
                upd = _dot_tn(blk(kd16, r)[:, ls], v_new16[:, ls])
                sp_ref[b * HEAD_PAIRS + p] = (s_old[p] * blk(s_decay, r)[0:1, ls]
                                              + jnp.where(mask_pair, upd, 0.0))
    o = jnp.concatenate(o_blocks, axis=0)

    ti = _iota((GM_CHUNK, GM_CHUNK), 0)
    tj = _iota((GM_CHUNK, GM_CHUNK), 1)
    y = None
    for p in range(GM_GROUPS // 2):
        pair = []
        for gi in range(2 * p, 2 * p + 2):
            c0 = gi * GM_GROUP_DIM
            ws_causal = _bf16(jnp.where(ti >= tj, ws_ref[gi], 0.0))
            rows = []
            for b in range(nb):
                rs = slice(b * SEQ_TILE, (b + 1) * SEQ_TILE)
                sp = _dot(ws_causal, _bf16(vln_ref[rs, c0:c0 + GM_GROUP_DIM])) + bs_ref[gi]
                rows.append(_bf16(ug_ref[rs, c0:c0 + GM_GROUP_DIM] * sp))
            pair.append(jnp.concatenate(rows, axis=0))
        part = _dot(jnp.concatenate(pair, axis=1),
                    wo_ref[DN_WIDTH + p * PAIR_WIDTH:DN_WIDTH + (p + 1) * PAIR_WIDTH, :])
        y = part if y is None else y + part

    for p in range(HEAD_PAIRS):
        pair = []
        for h in range(2 * p, 2 * p + 2):
            hs = slice(h * DN_HEAD, (h + 1) * DN_HEAD)
            oh = o[:, hs]
            oh = oh * lax.rsqrt(jnp.mean(oh * oh, axis=-1, keepdims=True) + EPS)
            pair.append(_bf16(oh * ong_ref[...] * _silu(z_ref[:, hs])))
        y = y + _dot(jnp.concatenate(pair, axis=1), wo_ref[p * PAIR_WIDTH:(p + 1) * PAIR_WIDTH, :])
    out_ref[...] = (x + y).reshape(nb, SEQ_TILE, D_MODEL)


def _layer_spec(shape, layer, n_grid):
    zeros = (0,) * len(shape)
    if n_grid == 2:
        return pl.BlockSpec((None,) + shape, lambda bi, ti: (layer,) + zeros)
    return pl.BlockSpec((None,) + shape, lambda i: (layer,) + zeros)


def _mixer(h, layer, nrm, w_pack, cw, alog, dtb, ong, lng, lnb, ws, bs_b, wo, w_gate, w_up, w_down):
    batch, seq, _ = h.shape
    nb = BATCH_TILE
    m = nb * SEQ_TILE
    seq_tiles = seq // SEQ_TILE
    n_steps = (batch // nb) * seq_tiles
    gate_rows = D_MODEL // n_steps
    down_rows = 2 * D_FF // n_steps
    assert gate_rows * n_steps == D_MODEL and gate_rows % BF16_SUBLANES == 0
    assert down_rows * n_steps == 2 * D_FF and down_rows % BF16_SUBLANES == 0
    spec = functools.partial(_layer_spec, layer=layer, n_grid=2)
    whole = lambda shape: pl.BlockSpec(shape, lambda bi, ti: (0,) * len(shape))
    step = lambda bi, ti: bi * seq_tiles + ti
    tile_spec = pl.BlockSpec((nb, SEQ_TILE, D_MODEL), lambda bi, ti: (bi, ti, 0))
    return pl.pallas_call(
        _mixer_kernel,
        out_shape=(jax.ShapeDtypeStruct(h.shape, h.dtype),
                   jax.ShapeDtypeStruct((D_MODEL, D_FF), jnp.bfloat16),
                   jax.ShapeDtypeStruct((D_MODEL, D_FF), jnp.bfloat16),
                   jax.ShapeDtypeStruct((D_FF, D_MODEL), jnp.bfloat16)),
        grid=(batch // nb, seq_tiles),
        in_specs=[
            tile_spec,
            spec((1, D_MODEL)),
            whole((D_MODEL, P_DIM)),
            spec((CONV_WIDTH, QKV_DIM)),
            spec((1, LANES)),
            spec((1, LANES)),
            spec((1, DN_HEAD)),
            spec((1, GM_WIDTH)),
            spec((1, GM_WIDTH)),
            spec((GM_GROUPS, GM_CHUNK, GM_CHUNK)),
            spec((GM_GROUPS, GM_CHUNK, GM_GROUP_DIM)),
            whole((D_MODEL, D_MODEL)),
            pl.BlockSpec((None, gate_rows, D_FF), lambda bi, ti: (layer, step(bi, ti), 0)),
            pl.BlockSpec((None, gate_rows, D_FF), lambda bi, ti: (layer, step(bi, ti), 0)),
            pl.BlockSpec((None, down_rows, D_MODEL), lambda bi, ti: (layer, step(bi, ti) // 2, 0)),
        ],
        out_specs=(
            tile_spec,
            pl.BlockSpec((gate_rows, D_FF), lambda bi, ti: (step(bi, ti), 0)),
            pl.BlockSpec((gate_rows, D_FF), lambda bi, ti: (step(bi, ti), 0)),
            pl.BlockSpec((down_rows, D_MODEL), lambda bi, ti: (step(bi, ti) // 2, 0)),
        ),
        scratch_shapes=[
            pltpu.VMEM((QKV_TILES, nb, HALO + SEQ_TILE, LANES), jnp.float32),
            pltpu.VMEM((m, QKV_DIM), jnp.float32),
            pltpu.VMEM((m, DN_WIDTH), jnp.float32),
            pltpu.VMEM((m, GM_WIDTH), jnp.float32),
            pltpu.VMEM((m, GM_WIDTH), jnp.float32),
            pltpu.VMEM((nb * HEAD_PAIRS, PAIR_WIDTH, PAIR_WIDTH), jnp.float32),
        ],
        compiler_params=pltpu.CompilerParams(
            dimension_semantics=("arbitrary", "arbitrary"),
            vmem_limit_bytes=VMEM_LIMIT_BYTES),
        name="mixer",
    )(h, nrm, w_pack, cw, alog, dtb, ong, lng, lnb, ws, bs_b, wo, w_gate, w_up, w_down)


def _pack_projection(w):
    lane = _iota((w.shape[0], LANES), 1)
    ba = jnp.where(lane < 2 * DN_HEADS, w[:, BETA_OFF:BETA_OFF + LANES], 0.0)
    return _bf16(jnp.concatenate([w[:, 0:Z_OFF], ba, w[:, GM_OFF:IN_DIM], w[:, Z_OFF:BETA_OFF]], axis=1))


def _ffn_kernel(*refs, final_norm, prep_next):
    if prep_next:
        (x_ref, nrm_ref, wg_ref, wu_ref, wd_ref, fin_ref, w_in_ref, w_out_ref,
         out_ref, w_pack_ref, wo16_ref) = refs
        w_pack_ref[...] = _pack_projection(w_in_ref[...])
        wo16_ref[...] = _bf16(w_out_ref[...])
    else:
        x_ref, nrm_ref, wg_ref, wu_ref, wd_ref, fin_ref, out_ref = refs
    x = x_ref[...]
    xn = _bf16(x * lax.rsqrt(jnp.mean(x * x, axis=-1, keepdims=True) + EPS) * nrm_ref[...])
    act = _bf16(_silu(_dot(xn, wg_ref[...])) * _dot(xn, wu_ref[...]))
    y = x + _dot(act, wd_ref[...])
    if final_norm:
        y = y * lax.rsqrt(jnp.mean(y * y, axis=-1, keepdims=True) + EPS) * fin_ref[...]
    out_ref[...] = y


def _ffn(h2, layer, nrm, wg, wu, wd, fin, final_norm, w_in=None, w_out=None):
    n = h2.shape[0]
    n_steps = n // FFN_ROWS
    prep_next = w_in is not None
    resident = lambda shape: pl.BlockSpec(shape, lambda i: (0, 0), pipeline_mode=pl.Buffered(1))
    row_spec = pl.BlockSpec((FFN_ROWS, D_MODEL), lambda i: (i, 0))
    in_specs = [
        row_spec,
        _layer_spec((1, D_MODEL), layer, 1),
        resident((D_MODEL, D_FF)),
        resident((D_MODEL, D_FF)),
        resident((D_FF, D_MODEL)),
        pl.BlockSpec((1, D_MODEL), lambda i: (0, 0)),
    ]
    args = [h2, nrm, wg, wu, wd, fin]
    out_shape = jax.ShapeDtypeStruct(h2.shape, h2.dtype)
    out_specs = row_spec
    if prep_next:
        slab = D_MODEL // n_steps
        assert slab * n_steps == D_MODEL and slab % BF16_SUBLANES == 0
        in_specs += [pl.BlockSpec((None, slab, IN_DIM), lambda i: (layer + 1, i, 0)),
                     pl.BlockSpec((None, slab, D_MODEL), lambda i: (layer + 1, i, 0))]
        args += [w_in, w_out]
        out_shape = (out_shape, jax.ShapeDtypeStruct((D_MODEL, P_DIM), jnp.bfloat16),
                     jax.ShapeDtypeStruct((D_MODEL, D_MODEL), jnp.bfloat16))
        out_specs = (row_spec, pl.BlockSpec((slab, P_DIM), lambda i: (i, 0)),
                     pl.BlockSpec((slab, D_MODEL), lambda i: (i, 0)))
    return pl.pallas_call(
        functools.partial(_ffn_kernel, final_norm=final_norm, prep_next=prep_next),
        out_shape=out_shape,
        grid=(n_steps,),
        in_specs=in_specs,
        out_specs=out_specs,
        compiler_params=pltpu.CompilerParams(
            dimension_semantics=("arbitrary",),
            vmem_limit_bytes=VMEM_LIMIT_BYTES),
        name="ffn",
    )(*args)


def _pad_lanes(v, offset):
    return jnp.pad(v, ((0, 0), (offset, LANES - offset - v.shape[1])))[:, None, :]


def kernel(x, norm_mix, w_in, conv_w, a_log, dt_bias, o_norm_g, ln_v_g, ln_v_b, w_s, b_s, w_out,
           norm_ffn, w_gate, w_up, w_down, norm_final):
    batch, seq, d = x.shape
    depth = w_in.shape[0]
    w_ba = jnp.pad(w_in[0][:, BETA_OFF:GM_OFF], ((0, 0), (0, LANES - 2 * DN_HEADS)))
    w_pack = _bf16(jnp.concatenate(
        [w_in[0][:, :Z_OFF], w_ba, w_in[0][:, GM_OFF:], w_in[0][:, Z_OFF:BETA_OFF]], axis=1))
    wo16 = _bf16(w_out[0])
    bs_b = jnp.broadcast_to(b_s[:, :, :, None], (depth, GM_GROUPS, GM_CHUNK, GM_GROUP_DIM))
    alog_p = _pad_lanes(a_log, DN_HEADS)
    dtb_p = _pad_lanes(dt_bias, DN_HEADS)
    row = lambda a: a[:, None, :]
    h = x
    for l in range(depth):
        h, wg16, wu16, wd16 = _mixer(h, l, row(norm_mix), w_pack, conv_w, alog_p, dtb_p, row(o_norm_g),
                                     row(ln_v_g), row(ln_v_b), w_s, bs_b, wo16, w_gate, w_up, w_down)
        h2 = h.reshape(batch * seq, d)
        if l + 1 < depth:
            h2, w_pack, wo16 = _ffn(h2, l, row(norm_ffn), wg16, wu16, wd16, norm_final[None, :], False,
                                    w_in, w_out)
        else:
            h2 = _ffn(h2, l, row(norm_ffn), wg16, wu16, wd16, norm_final[None, :], True)
        h = h2.reshape(batch, seq, d)
    return h
```

```python
import functools

import jax
import jax.numpy as jnp
from jax import lax
from jax.experimental import pallas as pl
from jax.experimental.pallas import tpu as pltpu

D_MODEL = 1024
DN_HEADS = 4
DN_HEAD = 128
DN_WIDTH = DN_HEADS * DN_HEAD
QKV_DIM = 3 * DN_WIDTH
CONV_WIDTH = 4
DN_CHUNK = 64
GM_GROUPS = 4
GM_GROUP_DIM = 128
GM_WIDTH = GM_GROUPS * GM_GROUP_DIM
GM_CHUNK = 128
D_FF = 2816
EPS = 1e-6

Z_OFF = QKV_DIM
BETA_OFF = Z_OFF + DN_WIDTH
GM_OFF = BETA_OFF + 2 * DN_HEADS
IN_DIM = GM_OFF + 2 * GM_WIDTH

LANES = 128
SUBLANES = 8
BF16_SUBLANES = 16

P_BA = QKV_DIM
P_GM = P_BA + LANES
P_Z = P_GM + 2 * GM_WIDTH
P_DIM = P_Z + DN_WIDTH

SEQ_TILE = GM_CHUNK
BATCH_TILE = 4
HALO = SUBLANES
QKV_TILES = QKV_DIM // LANES
FFN_ROWS = 512
PREP_ROWS = 128
VMEM_LIMIT_BYTES = 56 * 1024 * 1024

CHUNKS_PER_TILE = SEQ_TILE // DN_CHUNK
HEAD_PAIRS = DN_HEADS // 2
PAIR_WIDTH = 2 * DN_HEAD
MAT_WIDTH = DN_HEADS * DN_CHUNK
CHUNK_SHIFT = 6
HEAD_SHIFT = 7

GELU_C = 0.7978845608028654


def _dot(a, b):
    return jnp.dot(a, b, preferred_element_type=jnp.float32)


def _dot_nt(a, b):
    return lax.dot_general(a, b, (((1,), (1,)), ((), ())), preferred_element_type=jnp.float32)


def _dot_tn(a, b):
    return lax.dot_general(a, b, (((0,), (0,)), ((), ())), preferred_element_type=jnp.float32)


def _sigmoid(x):
    return 1.0 / (1.0 + jnp.exp(-x))


def _silu(x):
    return x * _sigmoid(x)


def _gelu_tanh(x):
    t = jnp.tanh(x * (GELU_C + (0.044715 * GELU_C) * (x * x)))
    return (0.5 * x) * (1.0 + t)


def _softplus(x):
    return jnp.maximum(x, 0.0) + jnp.log(1.0 + jnp.exp(-jnp.abs(x)))


def _bf16(x):
    return x.astype(jnp.bfloat16)


def _iota(shape, dim):
    return lax.broadcasted_iota(jnp.int32, shape, dim)


def _tile_rows(y, n):
    return jnp.concatenate([y] * n, axis=0)


def _mixer_kernel(x_ref, nrm_ref, w_ref, cw_ref, alog_ref, dtb_ref, ong_ref, lng_ref, lnb_ref,
                  ws_ref, bs_ref, wo_ref, wg_f32_ref, wu_f32_ref, wd_f32_ref,
                  out_ref, wg16_ref, wu16_ref, wd16_ref,
                  pre_ref, qkv_ref, z_ref, ug_ref, vln_ref, sp_ref):
    wg16_ref[...] = _bf16(wg_f32_ref[...])
    wu16_ref[...] = _bf16(wu_f32_ref[...])
    wd16_ref[...] = _bf16(wd_f32_ref[...])

    nb = x_ref.shape[0]
    m = nb * SEQ_TILE
    n_blk = m // DN_CHUNK
    t_idx = pl.program_id(1)

    @pl.when(t_idx == 0)
    def _():
        sp_ref[...] = jnp.zeros_like(sp_ref)
        pre_ref[:, :, 0:HALO, :] = jnp.zeros((QKV_TILES, nb, HALO, LANES), jnp.float32)

    x = x_ref[...].reshape(m, D_MODEL)
    xn = x * lax.rsqrt(jnp.mean(x * x, axis=-1, keepdims=True) + EPS) * nrm_ref[...]
    xn = _bf16(xn)

    proj = _dot(xn, w_ref[...])
    z_ref[...] = proj[:, P_Z:P_DIM]

    for j in range(QKV_TILES):
        c0 = j * LANES
        cur = proj[:, c0:c0 + LANES].reshape(nb, SEQ_TILE, LANES)
        pre_ref[j, :, HALO:HALO + SEQ_TILE, :] = cur
        acc = cur * cw_ref[CONV_WIDTH - 1:CONV_WIDTH, c0:c0 + LANES]
        for s in range(1, CONV_WIDTH):
            acc = acc + (pre_ref[j, :, HALO - s:HALO - s + SEQ_TILE, :]
                         * cw_ref[CONV_WIDTH - 1 - s:CONV_WIDTH - s, c0:c0 + LANES])
        y = _silu(acc).reshape(m, LANES)
        if j < 2 * DN_HEADS:
            y = y * lax.rsqrt(jnp.sum(y * y, axis=-1, keepdims=True) + EPS)
            if j < DN_HEADS:
                y = y * (DN_HEAD ** -0.5)
        qkv_ref[:, c0:c0 + LANES] = y
    pre_ref[:, :, 0:HALO, :] = pre_ref[:, :, SEQ_TILE:SEQ_TILE + HALO, :]

    for j in range(2 * GM_GROUPS):
        gm = _gelu_tanh(proj[:, P_GM + j * LANES:P_GM + (j + 1) * LANES])
        if j < GM_GROUPS:
            ug_ref[:, j * LANES:(j + 1) * LANES] = gm
        else:
            c0 = (j - GM_GROUPS) * LANES
            mu = jnp.mean(gm, axis=-1, keepdims=True)
            d = gm - mu
            var = jnp.mean(d * d, axis=-1, keepdims=True)
            vln_ref[:, c0:c0 + LANES] = (d * lax.rsqrt(var + EPS) * lng_ref[:, c0:c0 + LANES]
                                         + lnb_ref[:, c0:c0 + LANES])

    ba = proj[:, P_BA:P_BA + LANES]
    g = -jnp.exp(alog_ref[...]) * _softplus(ba + dtb_ref[...])
    pos = _iota((m, LANES), 0) & (DN_CHUNK - 1)
    shift = 1
    while shift < DN_CHUNK:
        g = g + jnp.where(pos >= shift, pltpu.roll(g, shift, axis=0), 0.0)
        shift *= 2
    gates = jnp.where(_iota((m, LANES), 1) < DN_HEADS, _sigmoid(ba), g)

    pi = _iota((m, MAT_WIDTH), 0) & (DN_CHUNK - 1)
    pj = _iota((m, MAT_WIDTH), 1) & (DN_CHUNK - 1)
    incl = pi >= pj
    strict = pi > pj
    eye4 = jnp.where(pi == pj, 1.0, 0.0).astype(jnp.float32)
    mask_mat = (_iota((MAT_WIDTH, MAT_WIDTH), 0) >> CHUNK_SHIFT) == (_iota((MAT_WIDTH, MAT_WIDTH), 1) >> CHUNK_SHIFT)
    mask_head = (_iota((MAT_WIDTH, DN_WIDTH), 0) >> CHUNK_SHIFT) == (_iota((MAT_WIDTH, DN_WIDTH), 1) >> HEAD_SHIFT)
    mask_pair = (_iota((PAIR_WIDTH, PAIR_WIDTH), 0) >> HEAD_SHIFT) == (_iota((PAIR_WIDTH, PAIR_WIDTH), 1) >> HEAD_SHIFT)

    def blk(a, r):
        return a[r * DN_CHUNK:(r + 1) * DN_CHUNK]

    bc = [jnp.broadcast_to(gates[:, c:c + 1], (m, LANES)) for c in range(2 * DN_HEADS)]
    low_half = _iota((m, LANES), 1) < DN_CHUNK

    def pair_cols(base):
        return jnp.concatenate([jnp.where(low_half, bc[base + 2 * v], bc[base + 2 * v + 1])
                                for v in range(HEAD_PAIRS)], axis=1)

    b_col = pair_cols(0)
    g_col = pair_cols(DN_HEADS)
    g_col_wide = jnp.concatenate(bc[DN_HEADS:2 * DN_HEADS], axis=1)

    low_half_row = _iota((1, LANES), 1) < DN_CHUNK
    b_row = []
    g_row = []
    for b in range(nb):
        gt = gates[b * SEQ_TILE:(b + 1) * SEQ_TILE, :].T[0:2 * DN_HEADS, :]
        gt_swapped = pltpu.roll(gt, DN_CHUNK, axis=1)
        for c in range(CHUNKS_PER_TILE):
            lo_src, hi_src = (gt, gt_swapped) if c == 0 else (gt_swapped, gt)

            def pair_rows(base):
                vec = jnp.concatenate(
                    [jnp.where(low_half_row, lo_src[base + 2 * v:base + 2 * v + 1, :],
                               hi_src[base + 2 * v + 1:base + 2 * v + 2, :])
                     for v in range(HEAD_PAIRS)], axis=1)
                return jnp.broadcast_to(vec, (DN_CHUNK, MAT_WIDTH))

            b_row.append(pair_rows(0))
            g_row.append(pair_rows(DN_HEADS))
    b_row = jnp.concatenate(b_row, axis=0)
    g_row = jnp.concatenate(g_row, axis=0)
    decay = jnp.where(incl, jnp.exp(jnp.where(incl, g_col - g_row, 0.0)), 0.0)

    q = qkv_ref[:, 0:DN_WIDTH]
    k = qkv_ref[:, DN_WIDTH:2 * DN_WIDTH]
    q16 = _bf16(q)
    k16 = _bf16(k)
    v16 = _bf16(qkv_ref[:, 2 * DN_WIDTH:3 * DN_WIDTH])

    k_bd = [jnp.where(mask_head, _tile_rows(blk(k16, r), DN_HEADS), 0.0) for r in range(n_blk)]
    kq = [_dot_nt(jnp.concatenate([blk(k16, r), blk(q16, r)], axis=0), k_bd[r]) for r in range(n_blk)]
    kk = jnp.concatenate([t[0:DN_CHUNK] for t in kq], axis=0)
    qk = jnp.concatenate([t[DN_CHUNK:2 * DN_CHUNK] for t in kq], axis=0)
    a_mat = jnp.where(strict, kk * b_col * decay, 0.0)
    qk16 = _bf16(qk * decay)

    def times(lhs16, x16):
        outs = [[] for _ in lhs16]
        for r in range(n_blk):
            rhs = jnp.where(mask_mat, _tile_rows(blk(x16, r), DN_HEADS), 0.0)
            res = _dot(jnp.concatenate([blk(l, r) for l in lhs16], axis=0), rhs)
            for i in range(len(lhs16)):
                outs[i].append(res[i * DN_CHUNK:(i + 1) * DN_CHUNK])
        return [jnp.concatenate(o, axis=0) for o in outs]

    pw = -a_mat
    inv = eye4 + pw
    pw16 = _bf16(pw)
    (pw,) = times([pw16], pw16)
    for _ in range(4):
        pw16 = _bf16(pw)
        pw, step = times([pw16, _bf16(inv)], pw16)
        inv = inv + step
    (step,) = times([_bf16(inv)], _bf16(pw))
    inv = inv + step
    inv_b = inv * b_row
    inv_b16 = _bf16(inv_b)
    inv_bg16 = _bf16(inv_b * jnp.exp(g_row))
    u = jnp.concatenate(
        [_dot(blk(inv_b16, r), jnp.where(mask_head, _tile_rows(blk(v16, r), DN_HEADS), 0.0))
         for r in range(n_blk)], axis=0)
    w16 = _bf16(jnp.concatenate([_dot(blk(inv_bg16, r), k_bd[r]) for r in range(n_blk)], axis=0))
    qg16 = _bf16(q * jnp.exp(g_col_wide))
    g_last = jnp.concatenate(
        [jnp.broadcast_to(blk(g_col_wide, r)[DN_CHUNK - 1:DN_CHUNK, :], (DN_CHUNK, DN_WIDTH))
         for r in range(n_blk)], axis=0)
    kd16 = _bf16(k * jnp.exp(g_last - g_col_wide))
    s_decay = jnp.exp(g_last)

    o_blocks = [None] * n_blk
    for c in range(CHUNKS_PER_TILE):
        for b in range(nb):
            r = b * CHUNKS_PER_TILE + c
            s_old = [sp_ref[b * HEAD_PAIRS + p] for p in range(HEAD_PAIRS)]
            v_new = []
            o_state = []
            for p in range(HEAD_PAIRS):
                ls = slice(p * PAIR_WIDTH, (p + 1) * PAIR_WIDTH)
                lhs = jnp.concatenate([blk(w16, r)[:, ls], blk(qg16, r)[:, ls]], axis=0)
                wq = _dot(lhs, _bf16(s_old[p]))
                v_new.append(blk(u, r)[:, ls] - wq[0:DN_CHUNK])
                o_state.append(wq[DN_CHUNK:2 * DN_CHUNK])
            v_new16 = _bf16(jnp.concatenate(v_new, axis=1))
            o_blocks[r] = jnp.concatenate(o_state, axis=1) + _dot(
                blk(qk16, r), jnp.where(mask_head, _tile_rows(v_new16, DN_HEADS), 0.0))
            for p in range(HEAD_PAIRS):
                ls = slice(p * PAIR_WIDTH, (p + 1) * PAIR_WIDTH)
                upd = _dot_tn(blk(kd16, r)[:, ls], v_new16[:, ls])
                sp_ref[b * HEAD_PAIRS + p] = (s_old[p] * blk(s_decay, r)[0:1, ls]
                                              + jnp.where(mask_pair, upd, 0.0))
    o = jnp.concatenate(o_blocks, axis=0)

    ti = _iota((GM_CHUNK, GM_CHUNK), 0)
    tj = _iota((GM_CHUNK, GM_CHUNK), 1)
    y = None
    for p in range(GM_GROUPS // 2):
        pair = []
        for gi in range(2 * p, 2 * p + 2):
            c0 = gi * GM_GROUP_DIM
            ws_causal = _bf16(jnp.where(ti >= tj, ws_ref[gi], 0.0))
            rows = []
            for b in range(nb):
                rs = slice(b * SEQ_TILE, (b + 1) * SEQ_TILE)
                sp = _dot(ws_causal, _bf16(vln_ref[rs, c0:c0 + GM_GROUP_DIM])) + bs_ref[gi]
                rows.append(_bf16(ug_ref[rs, c0:c0 + GM_GROUP_DIM] * sp))
            pair.append(jnp.concatenate(rows, axis=0))
        part = _dot(jnp.concatenate(pair, axis=1),
                    wo_ref[DN_WIDTH + p * PAIR_WIDTH:DN_WIDTH + (p + 1) * PAIR_WIDTH, :])
        y = part if y is None else y + part

    for p in range(HEAD_PAIRS):
        pair = []
        for h in range(2 * p, 2 * p + 2):
            hs = slice(h * DN_HEAD, (h + 1) * DN_HEAD)
            oh = o[:, hs]
            oh = oh * lax.rsqrt(jnp.mean(oh * oh, axis=-1, keepdims=True) + EPS)
            pair.append(_bf16(oh * ong_ref[...] * _silu(z_ref[:, hs])))
        y = y + _dot(jnp.concatenate(pair, axis=1), wo_ref[p * PAIR_WIDTH:(p + 1) * PAIR_WIDTH, :])
    out_ref[...] = (x + y).reshape(nb, SEQ_TILE, D_MODEL)


def _layer_spec(shape, layer, n_grid):
    zeros = (0,) * len(shape)
    if n_grid == 2:
        return pl.BlockSpec((None,) + shape, lambda bi, ti: (layer,) + zeros)
    return pl.BlockSpec((None,) + shape, lambda i: (layer,) + zeros)


def _mixer(h, layer, nrm, w_pack, cw, alog, dtb, ong, lng, lnb, ws, bs_b, wo, w_gate, w_up, w_down):
    batch, seq, _ = h.shape
    nb = BATCH_TILE
    m = nb * SEQ_TILE
    seq_tiles = seq // SEQ_TILE
    n_steps = (batch // nb) * seq_tiles
    gate_rows = D_MODEL // n_steps
    down_rows = 2 * D_FF // n_steps
    assert gate_rows * n_steps == D_MODEL and gate_rows % BF16_SUBLANES == 0
    assert down_rows * n_steps == 2 * D_FF and down_rows % BF16_SUBLANES == 0
    spec = functools.partial(_layer_spec, layer=layer, n_grid=2)
    whole = lambda shape: pl.BlockSpec(shape, lambda bi, ti: (0,) * len(shape))
    step = lambda bi, ti: bi * seq_tiles + ti
    tile_spec = pl.BlockSpec((nb, SEQ_TILE, D_MODEL), lambda bi, ti: (bi, ti, 0))
    return pl.pallas_call(
        _mixer_kernel,
        out_shape=(jax.ShapeDtypeStruct(h.shape, h.dtype),
                   jax.ShapeDtypeStruct((D_MODEL, D_FF), jnp.bfloat16),
                   jax.ShapeDtypeStruct((D_MODEL, D_FF), jnp.bfloat16),
                   jax.ShapeDtypeStruct((D_FF, D_MODEL), jnp.bfloat16)),
        grid=(batch // nb, seq_tiles),
        in_specs=[
            tile_spec,
            spec((1, D_MODEL)),
            whole((D_MODEL, P_DIM)),
            spec((CONV_WIDTH, QKV_DIM)),
            spec((1, LANES)),
            spec((1, LANES)),
            spec((1, DN_HEAD)),
            spec((1, GM_WIDTH)),
            spec((1, GM_WIDTH)),
            spec((GM_GROUPS, GM_CHUNK, GM_CHUNK)),
            spec((GM_GROUPS, GM_CHUNK, GM_GROUP_DIM)),
            whole((D_MODEL, D_MODEL)),
            pl.BlockSpec((None, gate_rows, D_FF), lambda bi, ti: (layer, step(bi, ti), 0)),
            pl.BlockSpec((None, gate_rows, D_FF), lambda bi, ti: (layer, step(bi, ti), 0)),
            pl.BlockSpec((None, down_rows, D_MODEL), lambda bi, ti: (layer, step(bi, ti) // 2, 0)),
        ],
        out_specs=(
            tile_spec,
            pl.BlockSpec((gate_rows, D_FF), lambda bi, ti: (step(bi, ti), 0)),
            pl.BlockSpec((gate_rows, D_FF), lambda bi, ti: (step(bi, ti), 0)),
            pl.BlockSpec((down_rows, D_MODEL), lambda bi, ti: (step(bi, ti) // 2, 0)),
        ),
        scratch_shapes=[
            pltpu.VMEM((QKV_TILES, nb, HALO + SEQ_TILE, LANES), jnp.float32),
            pltpu.VMEM((m, QKV_DIM), jnp.float32),
            pltpu.VMEM((m, DN_WIDTH), jnp.float32),
            pltpu.VMEM((m, GM_WIDTH), jnp.float32),
            pltpu.VMEM((m, GM_WIDTH), jnp.float32),
            pltpu.VMEM((nb * HEAD_PAIRS, PAIR_WIDTH, PAIR_WIDTH), jnp.float32),
        ],
        compiler_params=pltpu.CompilerParams(
            dimension_semantics=("arbitrary", "arbitrary"),
            vmem_limit_bytes=VMEM_LIMIT_BYTES),
        name="mixer",
    )(h, nrm, w_pack, cw, alog, dtb, ong, lng, lnb, ws, bs_b, wo, w_gate, w_up, w_down)


def _pack_projection(w):
    lane = _iota((w.shape[0], LANES), 1)
    ba = jnp.where(lane < 2 * DN_HEADS, w[:, BETA_OFF:BETA_OFF + LANES], 0.0)
    return _bf16(jnp.concatenate([w[:, 0:Z_OFF], ba, w[:, GM_OFF:IN_DIM], w[:, Z_OFF:BETA_OFF]], axis=1))


def _prep_kernel(w_in_ref, w_out_ref, w_pack_ref, wo16_ref):
    w_pack_ref[...] = _pack_projection(w_in_ref[...])
    wo16_ref[...] = _bf16(w_out_ref[...])


def _prep_first_layer(w_in, w_out):
    return pl.pallas_call(
        _prep_kernel,
        out_shape=(jax.ShapeDtypeStruct((D_MODEL, P_DIM), jnp.bfloat16),
                   jax.ShapeDtypeStruct((D_MODEL, D_MODEL), jnp.bfloat16)),
        grid=(D_MODEL // PREP_ROWS,),
        in_specs=[pl.BlockSpec((None, PREP_ROWS, IN_DIM), lambda i: (0, i, 0)),
                  pl.BlockSpec((None, PREP_ROWS, D_MODEL), lambda i: (0, i, 0))],
        out_specs=(pl.BlockSpec((PREP_ROWS, P_DIM), lambda i: (i, 0)),
                   pl.BlockSpec((PREP_ROWS, D_MODEL), lambda i: (i, 0))),
        compiler_params=pltpu.CompilerParams(dimension_semantics=("arbitrary",)),
        name="prep",
    )(w_in, w_out)


def _ffn_kernel(*refs, final_norm, prep_next):
    if prep_next:
        (x_ref, nrm_ref, wg_ref, wu_ref, wd_ref, fin_ref, w_in_ref, w_out_ref,
         out_ref, w_pack_ref, wo16_ref) = refs
        w_pack_ref[...] = _pack_projection(w_in_ref[...])
        wo16_ref[...] = _bf16(w_out_ref[...])
    else:
        x_ref, nrm_ref, wg_ref, wu_ref, wd_ref, fin_ref, out_ref = refs
    x = x_ref[...]
    xn = _bf16(x * lax.rsqrt(jnp.mean(x * x, axis=-1, keepdims=True) + EPS) * nrm_ref[...])
    act = _bf16(_silu(_dot(xn, wg_ref[...])) * _dot(xn, wu_ref[...]))
    y = x + _dot(act, wd_ref[...])
    if final_norm:
        y = y * lax.rsqrt(jnp.mean(y * y, axis=-1, keepdims=True) + EPS) * fin_ref[...]
    out_ref[...] = y


def _ffn(h2, layer, nrm, wg, wu, wd, fin, final_norm, w_in=None, w_out=None):
    n = h2.shape[0]
    n_steps = n // FFN_ROWS
    prep_next = w_in is not None
    resident = lambda shape: pl.BlockSpec(shape, lambda i: (0, 0), pipeline_mode=pl.Buffered(1))
    row_spec = pl.BlockSpec((FFN_ROWS, D_MODEL), lambda i: (i, 0))
    in_specs = [
        row_spec,
        _layer_spec((1, D_MODEL), layer, 1),
        resident((D_MODEL, D_FF)),
        resident((D_MODEL, D_FF)),
        resident((D_FF, D_MODEL)),
        pl.BlockSpec((1, D_MODEL), lambda i: (0, 0)),
    ]
    args = [h2, nrm, wg, wu, wd, fin]
    out_shape = jax.ShapeDtypeStruct(h2.shape, h2.dtype)
    out_specs = row_spec
    if prep_next:
        slab = D_MODEL // n_steps
        assert slab * n_steps == D_MODEL and slab % BF16_SUBLANES == 0
        in_specs += [pl.BlockSpec((None, slab, IN_DIM), lambda i: (layer + 1, i, 0)),
                     pl.BlockSpec((None, slab, D_MODEL), lambda i: (layer + 1, i, 0))]
        args += [w_in, w_out]
        out_shape = (out_shape, jax.ShapeDtypeStruct((D_MODEL, P_DIM), jnp.bfloat16),
                     jax.ShapeDtypeStruct((D_MODEL, D_MODEL), jnp.bfloat16))
        out_specs = (row_spec, pl.BlockSpec((slab, P_DIM), lambda i: (i, 0)),
                     pl.BlockSpec((slab, D_MODEL), lambda i: (i, 0)))
    return pl.pallas_call(
        functools.partial(_ffn_kernel, final_norm=final_norm, prep_next=prep_next),
        out_shape=out_shape,
        grid=(n_steps,),
        in_specs=in_specs,
        out_specs=out_specs,
        compiler_params=pltpu.CompilerParams(
            dimension_semantics=("arbitrary",),
            vmem_limit_bytes=VMEM_LIMIT_BYTES),
        name="ffn",
    )(*args)


def _pad_lanes(v, offset):
    return jnp.pad(v, ((0, 0), (offset, LANES - offset - v.shape[1])))[:, None, :]


def kernel(x, norm_mix, w_in, conv_w, a_log, dt_bias, o_norm_g, ln_v_g, ln_v_b, w_s, b_s, w_out,
           norm_ffn, w_gate, w_up, w_down, norm_final):
    batch, seq, d = x.shape
    depth = w_in.shape[0]
    w_in16 = _bf16(w_in)
    w_pack, wo16 = _prep_first_layer(w_in16, w_out)
    bs_b = jnp.broadcast_to(b_s[:, :, :, None], (depth, GM_GROUPS, GM_CHUNK, GM_GROUP_DIM))
    alog_p = _pad_lanes(a_log, DN_HEADS)
    dtb_p = _pad_lanes(dt_bias, DN_HEADS)
    row = lambda a: a[:, None, :]
    h = x
    for l in range(depth):
        h, wg16, wu16, wd16 = _mixer(h, l, row(norm_mix), w_pack, conv_w, alog_p, dtb_p, row(o_norm_g),
                                     row(ln_v_g), row(ln_v_b), w_s, bs_b, wo16, w_gate, w_up, w_down)
        h2 = h.reshape(batch * seq, d)
        if l + 1 < depth:
            h2, w_pack, wo16 = _ffn(h2, l, row(norm_ffn), wg16, wu16, wd16, norm_final[None, :], False,
                                    w_in16, w_out)
        else:
            h2 = _ffn(h2, l, row(norm_ffn), wg16, wu16, wd16, norm_final[None, :], True)
        h = h2.reshape(batch, seq, d)
    return h
```

```python
import functools

import jax
import jax.numpy as jnp
from jax import lax
from jax.experimental import pallas as pl
from jax.experimental.pallas import tpu as pltpu

D_MODEL = 1024
DN_HEADS = 4
DN_HEAD = 128
DN_WIDTH = DN_HEADS * DN_HEAD
QKV_DIM = 3 * DN_WIDTH
CONV_WIDTH = 4
DN_CHUNK = 64
GM_GROUPS = 4
GM_GROUP_DIM = 128
GM_WIDTH = GM_GROUPS * GM_GROUP_DIM
GM_CHUNK = 128
D_FF = 2816
EPS = 1e-6

Z_OFF = QKV_DIM
BETA_OFF = Z_OFF + DN_WIDTH
GM_OFF = BETA_OFF + 2 * DN_HEADS
IN_DIM = GM_OFF + 2 * GM_WIDTH

LANES = 128
SUBLANES = 8
BF16_SUBLANES = 16

P_BA = QKV_DIM
P_GM = P_BA + LANES
P_Z = P_GM + 2 * GM_WIDTH
P_DIM = P_Z + DN_WIDTH

SEQ_TILE = GM_CHUNK
BATCH_TILE = 4
TILES_PER_STEP = 2
HALO = SUBLANES
QKV_TILES = QKV_DIM // LANES
FFN_ROWS = 512
PREP_ROWS = 128
VMEM_LIMIT_BYTES = 56 * 1024 * 1024

CHUNKS_PER_TILE = SEQ_TILE // DN_CHUNK
HEAD_PAIRS = DN_HEADS // 2
PAIR_WIDTH = 2 * DN_HEAD
MAT_WIDTH = DN_HEADS * DN_CHUNK
CHUNK_SHIFT = 6
HEAD_SHIFT = 7

GELU_C = 0.7978845608028654


def _dot(a, b):
    return jnp.dot(a, b, preferred_element_type=jnp.float32)


def _dot_nt(a, b):
    return lax.dot_general(a, b, (((1,), (1,)), ((), ())), preferred_element_type=jnp.float32)


def _dot_tn(a, b):
    return lax.dot_general(a, b, (((0,), (0,)), ((), ())), preferred_element_type=jnp.float32)


def _sigmoid(x):
    return 1.0 / (1.0 + jnp.exp(-x))


def _silu(x):
    return x * _sigmoid(x)


def _gelu_tanh(x):
    t = jnp.tanh(x * (GELU_C + (0.044715 * GELU_C) * (x * x)))
    return (0.5 * x) * (1.0 + t)


def _softplus(x):
    return jnp.maximum(x, 0.0) + jnp.log(1.0 + jnp.exp(-jnp.abs(x)))


def _bf16(x):
    return x.astype(jnp.bfloat16)


def _iota(shape, dim):
    return lax.broadcasted_iota(jnp.int32, shape, dim)


def _tile_rows(y, n):
    return jnp.concatenate([y] * n, axis=0)


def _mixer_tile(x_ref, nrm_ref, w_ref, cw_ref, alog_ref, dtb_ref, ong_ref, lng_ref, lnb_ref,
                ws_ref, bs_ref, wo_ref, out_ref, pre_ref, qkv_ref, z_ref, ug_ref, vln_ref, sp_ref):
    nb = x_ref.shape[0]
    m = nb * SEQ_TILE
    n_blk = m // DN_CHUNK

    x = x_ref[...].reshape(m, D_MODEL)
    xn = x * lax.rsqrt(jnp.mean(x * x, axis=-1, keepdims=True) + EPS) * nrm_ref[...]
    xn = _bf16(xn)

    proj = _dot(xn, w_ref[...])
    z_ref[...] = proj[:, P_Z:P_DIM]

    for j in range(QKV_TILES):
        c0 = j * LANES
        cur = proj[:, c0:c0 + LANES].reshape(nb, SEQ_TILE, LANES)
        pre_ref[j, :, HALO:HALO + SEQ_TILE, :] = cur
        acc = cur * cw_ref[CONV_WIDTH - 1:CONV_WIDTH, c0:c0 + LANES]
        for s in range(1, CONV_WIDTH):
            acc = acc + (pre_ref[j, :, HALO - s:HALO - s + SEQ_TILE, :]
                         * cw_ref[CONV_WIDTH - 1 - s:CONV_WIDTH - s, c0:c0 + LANES])
        y = _silu(acc).reshape(m, LANES)
        if j < 2 * DN_HEADS:
            y = y * lax.rsqrt(jnp.sum(y * y, axis=-1, keepdims=True) + EPS)
            if j < DN_HEADS:
                y = y * (DN_HEAD ** -0.5)
        qkv_ref[:, c0:c0 + LANES] = y
    pre_ref[:, :, 0:HALO, :] = pre_ref[:, :, SEQ_TILE:SEQ_TILE + HALO, :]

    for j in range(2 * GM_GROUPS):
        gm = _gelu_tanh(proj[:, P_GM + j * LANES:P_GM + (j + 1) * LANES])
        if j < GM_GROUPS:
            ug_ref[:, j * LANES:(j + 1) * LANES] = gm
        else:
            c0 = (j - GM_GROUPS) * LANES
            mu = jnp.mean(gm, axis=-1, keepdims=True)
            d = gm - mu
            var = jnp.mean(d * d, axis=-1, keepdims=True)
            vln_ref[:, c0:c0 + LANES] = (d * lax.rsqrt(var + EPS) * lng_ref[:, c0:c0 + LANES]
                                         + lnb_ref[:, c0:c0 + LANES])

    ba = proj[:, P_BA:P_BA + LANES]
    g = -jnp.exp(alog_ref[...]) * _softplus(ba + dtb_ref[...])
    pos = _iota((m, LANES), 0) & (DN_CHUNK - 1)
    shift = 1
    while shift < DN_CHUNK:
        g = g + jnp.where(pos >= shift, pltpu.roll(g, shift, axis=0), 0.0)
        shift *= 2
    gates = jnp.where(_iota((m, LANES), 1) < DN_HEADS, _sigmoid(ba), g)

    pi = _iota((m, MAT_WIDTH), 0) & (DN_CHUNK - 1)
    pj = _iota((m, MAT_WIDTH), 1) & (DN_CHUNK - 1)
    incl = pi >= pj
    strict = pi > pj
    eye4 = jnp.where(pi == pj, 1.0, 0.0).astype(jnp.float32)
    mask_mat = (_iota((MAT_WIDTH, MAT_WIDTH), 0) >> CHUNK_SHIFT) == (_iota((MAT_WIDTH, MAT_WIDTH), 1) >> CHUNK_SHIFT)
    mask_head = (_iota((MAT_WIDTH, DN_WIDTH), 0) >> CHUNK_SHIFT) == (_iota((MAT_WIDTH, DN_WIDTH), 1) >> HEAD_SHIFT)
    mask_pair = (_iota((PAIR_WIDTH, PAIR_WIDTH), 0) >> HEAD_SHIFT) == (_iota((PAIR_WIDTH, PAIR_WIDTH), 1) >> HEAD_SHIFT)

    def blk(a, r):
        return a[r * DN_CHUNK:(r + 1) * DN_CHUNK]

    bc = [jnp.broadcast_to(gates[:, c:c + 1], (m, LANES)) for c in range(2 * DN_HEADS)]
    low_half = _iota((m, LANES), 1) < DN_CHUNK

    def pair_cols(base):
        return jnp.concatenate([jnp.where(low_half, bc[base + 2 * v], bc[base + 2 * v + 1])
                                for v in range(HEAD_PAIRS)], axis=1)

    b_col = pair_cols(0)
    g_col = pair_cols(DN_HEADS)
    g_col_wide = jnp.concatenate(bc[DN_HEADS:2 * DN_HEADS], axis=1)

    low_half_row = _iota((1, LANES), 1) < DN_CHUNK
    b_row = []
    g_row = []
    for b in range(nb):
        gt = gates[b * SEQ_TILE:(b + 1) * SEQ_TILE, :].T[0:2 * DN_HEADS, :]
        gt_swapped = pltpu.roll(gt, DN_CHUNK, axis=1)
        for c in range(CHUNKS_PER_TILE):
            lo_src, hi_src = (gt, gt_swapped) if c == 0 else (gt_swapped, gt)

            def pair_rows(base):
                vec = jnp.concatenate(
                    [jnp.where(low_half_row, lo_src[base + 2 * v:base + 2 * v + 1, :],
                               hi_src[base + 2 * v + 1:base + 2 * v + 2, :])
                     for v in range(HEAD_PAIRS)], axis=1)
                return jnp.broadcast_to(vec, (DN_CHUNK, MAT_WIDTH))

            b_row.append(pair_rows(0))
            g_row.append(pair_rows(DN_HEADS))
    b_row = jnp.concatenate(b_row, axis=0)
    g_row = jnp.concatenate(g_row, axis=0)
    decay = jnp.where(incl, jnp.exp(jnp.where(incl, g_col - g_row, 0.0)), 0.0)

    q = qkv_ref[:, 0:DN_WIDTH]
    k = qkv_ref[:, DN_WIDTH:2 * DN_WIDTH]
    q16 = _bf16(q)
    k16 = _bf16(k)
    v16 = _bf16(qkv_ref[:, 2 * DN_WIDTH:3 * DN_WIDTH])

    k_bd = [jnp.where(mask_head, _tile_rows(blk(k16, r), DN_HEADS), 0.0) for r in range(n_blk)]
    kq = [_dot_nt(jnp.concatenate([blk(k16, r), blk(q16, r)], axis=0), k_bd[r]) for r in range(n_blk)]
    kk = jnp.concatenate([t[0:DN_CHUNK] for t in kq], axis=0)
    qk = jnp.concatenate([t[DN_CHUNK:2 * DN_CHUNK] for t in kq], axis=0)
    a_mat = jnp.where(strict, kk * b_col * decay, 0.0)
    qk16 = _bf16(qk * decay)

    def times(lhs16, x16):
        outs = [[] for _ in lhs16]
        for r in range(n_blk):
            rhs = jnp.where(mask_mat, _tile_rows(blk(x16, r), DN_HEADS), 0.0)
            res = _dot(jnp.concatenate([blk(l, r) for l in lhs16], axis=0), rhs)
            for i in range(len(lhs16)):
                outs[i].append(res[i * DN_CHUNK:(i + 1) * DN_CHUNK])
        return [jnp.concatenate(o, axis=0) for o in outs]

    pw = -a_mat
    inv = eye4 + pw
    pw16 = _bf16(pw)
    (pw,) = times([pw16], pw16)
    for _ in range(4):
        pw16 = _bf16(pw)
        pw, step = times([pw16, _bf16(inv)], pw16)
        inv = inv + step
    (step,) = times([_bf16(inv)], _bf16(pw))
    inv = inv + step
    inv_b = inv * b_row
    inv_b16 = _bf16(inv_b)
    inv_bg16 = _bf16(inv_b * jnp.exp(g_row))
    u = jnp.concatenate(
        [_dot(blk(inv_b16, r), jnp.where(mask_head, _tile_rows(blk(v16, r), DN_HEADS), 0.0))
         for r in range(n_blk)], axis=0)
    w16 = _bf16(jnp.concatenate([_dot(blk(inv_bg16, r), k_bd[r]) for r in range(n_blk)], axis=0))
    qg16 = _bf16(q * jnp.exp(g_col_wide))
    g_last = jnp.concatenate(
        [jnp.broadcast_to(blk(g_col_wide, r)[DN_CHUNK - 1:DN_CHUNK, :], (DN_CHUNK, DN_WIDTH))
         for r in range(n_blk)], axis=0)
    kd16 = _bf16(k * jnp.exp(g_last - g_col_wide))
    s_decay = jnp.exp(g_last)

    o_blocks = [None] * n_blk
    for c in range(CHUNKS_PER_TILE):
        for b in range(nb):
            r = b * CHUNKS_PER_TILE + c
            s_old = [sp_ref[b * HEAD_PAIRS + p] for p in range(HEAD_PAIRS)]
            v_new = []
            o_state = []
            for p in range(HEAD_PAIRS):
                ls = slice(p * PAIR_WIDTH, (p + 1) * PAIR_WIDTH)
                lhs = jnp.concatenate([blk(w16, r)[:, ls], blk(qg16, r)[:, ls]], axis=0)
                wq = _dot(lhs, _bf16(s_old[p]))
                v_new.append(blk(u, r)[:, ls] - wq[0:DN_CHUNK])
                o_state.append(wq[DN_CHUNK:2 * DN_CHUNK])
            v_new16 = _bf16(jnp.concatenate(v_new, axis=1))
            o_blocks[r] = jnp.concatenate(o_state, axis=1) + _dot(
                blk(qk16, r), jnp.where(mask_head, _tile_rows(v_new16, DN_HEADS), 0.0))
            for p in range(HEAD_PAIRS):
                ls = slice(p * PAIR_WIDTH, (p + 1) * PAIR_WIDTH)
                upd = _dot_tn(blk(kd16, r)[:, ls], v_new16[:, ls])
                sp_ref[b * HEAD_PAIRS + p] = (s_old[p] * blk(s_decay, r)[0:1, ls]
                                              + jnp.where(mask_pair, upd, 0.0))
    o = jnp.concatenate(o_blocks, axis=0)

    ti = _iota((GM_CHUNK, GM_CHUNK), 0)
    tj = _iota((GM_CHUNK, GM_CHUNK), 1)
    y = None
    for p in range(GM_GROUPS // 2):
        pair = []
        for gi in range(2 * p, 2 * p + 2):
            c0 = gi * GM_GROUP_DIM
            ws_causal = _bf16(jnp.where(ti >= tj, ws_ref[gi], 0.0))
            rows = []
            for b in range(nb):
                rs = slice(b * SEQ_TILE, (b + 1) * SEQ_TILE)
                sp = _dot(ws_causal, _bf16(vln_ref[rs, c0:c0 + GM_GROUP_DIM])) + bs_ref[gi]
                rows.append(_bf16(ug_ref[rs, c0:c0 + GM_GROUP_DIM] * sp))
            pair.append(jnp.concatenate(rows, axis=0))
        part = _dot(jnp.concatenate(pair, axis=1),
                    wo_ref[DN_WIDTH + p * PAIR_WIDTH:DN_WIDTH + (p + 1) * PAIR_WIDTH, :])
        y = part if y is None else y + part

    for p in range(HEAD_PAIRS):
        pair = []
        for h in range(2 * p, 2 * p + 2):
            hs = slice(h * DN_HEAD, (h + 1) * DN_HEAD)
            oh = o[:, hs]
            oh = oh * lax.rsqrt(jnp.mean(oh * oh, axis=-1, keepdims=True) + EPS)
            pair.append(_bf16(oh * ong_ref[...] * _silu(z_ref[:, hs])))
        y = y + _dot(jnp.concatenate(pair, axis=1), wo_ref[p * PAIR_WIDTH:(p + 1) * PAIR_WIDTH, :])
    out_ref[...] = (x + y).reshape(nb, SEQ_TILE, D_MODEL)


def _mixer_kernel(x_ref, nrm_ref, w_ref, cw_ref, alog_ref, dtb_ref, ong_ref, lng_ref, lnb_ref,
                  ws_ref, bs_ref, wo_ref, wg_f32_ref, wu_f32_ref, wd_f32_ref,
                  out_ref, wg16_ref, wu16_ref, wd16_ref,
                  pre_ref, qkv_ref, z_ref, ug_ref, vln_ref, sp_ref):
    wg16_ref[...] = _bf16(wg_f32_ref[...])
    wu16_ref[...] = _bf16(wu_f32_ref[...])
    wd16_ref[...] = _bf16(wd_f32_ref[...])

    @pl.when(pl.program_id(1) == 0)
    def _():
        sp_ref[...] = jnp.zeros_like(sp_ref)
        pre_ref[:, :, 0:HALO, :] = jnp.zeros((QKV_TILES, x_ref.shape[0], HALO, LANES), jnp.float32)

    for i in range(TILES_PER_STEP):
        rows = slice(i * SEQ_TILE, (i + 1) * SEQ_TILE)
        _mixer_tile(x_ref.at[:, rows, :], nrm_ref, w_ref, cw_ref, alog_ref, dtb_ref, ong_ref, lng_ref,
                    lnb_ref, ws_ref, bs_ref, wo_ref, out_ref.at[:, rows, :],
                    pre_ref, qkv_ref, z_ref, ug_ref, vln_ref, sp_ref)


def _layer_spec(shape, layer, n_grid):
    zeros = (0,) * len(shape)
    if n_grid == 2:
        return pl.BlockSpec((None,) + shape, lambda bi, ti: (layer,) + zeros)
    return pl.BlockSpec((None,) + shape, lambda i: (layer,) + zeros)


def _mixer(h, layer, nrm, w_pack, cw, alog, dtb, ong, lng, lnb, ws, bs_b, wo, w_gate, w_up, w_down):
    batch, seq, _ = h.shape
    nb = BATCH_TILE
    m = nb * SEQ_TILE
    seq_tiles = seq // (SEQ_TILE * TILES_PER_STEP)
    n_steps = (batch // nb) * seq_tiles
    gate_rows = D_MODEL // n_steps
    down_rows = D_FF // n_steps
    assert gate_rows * n_steps == D_MODEL and gate_rows % BF16_SUBLANES == 0
    assert down_rows * n_steps == D_FF and down_rows % BF16_SUBLANES == 0
    spec = functools.partial(_layer_spec, layer=layer, n_grid=2)
    whole = lambda shape: pl.BlockSpec(shape, lambda bi, ti: (0,) * len(shape))
    step = lambda bi, ti: bi * seq_tiles + ti
    tile_spec = pl.BlockSpec((nb, SEQ_TILE * TILES_PER_STEP, D_MODEL), lambda bi, ti: (bi, ti, 0))
    return pl.pallas_call(
        _mixer_kernel,
        out_shape=(jax.ShapeDtypeStruct(h.shape, h.dtype),
                   jax.ShapeDtypeStruct((D_MODEL, D_FF), jnp.bfloat16),
                   jax.ShapeDtypeStruct((D_MODEL, D_FF), jnp.bfloat16),
                   jax.ShapeDtypeStruct((D_FF, D_MODEL), jnp.bfloat16)),
        grid=(batch // nb, seq_tiles),
        in_specs=[
            tile_spec,
            spec((1, D_MODEL)),
            whole((D_MODEL, P_DIM)),
            spec((CONV_WIDTH, QKV_DIM)),
            spec((1, LANES)),
            spec((1, LANES)),
            spec((1, DN_HEAD)),
            spec((1, GM_WIDTH)),
            spec((1, GM_WIDTH)),
            spec((GM_GROUPS, GM_CHUNK, GM_CHUNK)),
            spec((GM_GROUPS, GM_CHUNK, GM_GROUP_DIM)),
            whole((D_MODEL, D_MODEL)),
            pl.BlockSpec((None, gate_rows, D_FF), lambda bi, ti: (layer, step(bi, ti), 0)),
            pl.BlockSpec((None, gate_rows, D_FF), lambda bi, ti: (layer, step(bi, ti), 0)),
            pl.BlockSpec((None, down_rows, D_MODEL), lambda bi, ti: (layer, step(bi, ti), 0)),
        ],
        out_specs=(
            tile_spec,
            pl.BlockSpec((gate_rows, D_FF), lambda bi, ti: (step(bi, ti), 0)),
            pl.BlockSpec((gate_rows, D_FF), lambda bi, ti: (step(bi, ti), 0)),
            pl.BlockSpec((down_rows, D_MODEL), lambda bi, ti: (step(bi, ti), 0)),
        ),
        scratch_shapes=[
            pltpu.VMEM((QKV_TILES, nb, HALO + SEQ_TILE, LANES), jnp.float32),
            pltpu.VMEM((m, QKV_DIM), jnp.float32),
            pltpu.VMEM((m, DN_WIDTH), jnp.float32),
            pltpu.VMEM((m, GM_WIDTH), jnp.float32),
            pltpu.VMEM((m, GM_WIDTH), jnp.float32),
            pltpu.VMEM((nb * HEAD_PAIRS, PAIR_WIDTH, PAIR_WIDTH), jnp.float32),
        ],
        compiler_params=pltpu.CompilerParams(
            dimension_semantics=("arbitrary", "arbitrary"),
            vmem_limit_bytes=VMEM_LIMIT_BYTES),
        name="mixer",
    )(h, nrm, w_pack, cw, alog, dtb, ong, lng, lnb, ws, bs_b, wo, w_gate, w_up, w_down)


def _pack_projection(w):
    lane = _iota((w.shape[0], LANES), 1)
    ba = jnp.where(lane < 2 * DN_HEADS, w[:, BETA_OFF:BETA_OFF + LANES], 0.0)
    return _bf16(jnp.concatenate([w[:, 0:Z_OFF], ba, w[:, GM_OFF:IN_DIM], w[:, Z_OFF:BETA_OFF]], axis=1))


def _prep_kernel(w_in_ref, w_out_ref, w_pack_ref, wo16_ref):
    w_pack_ref[...] = _pack_projection(w_in_ref[...])
    wo16_ref[...] = _bf16(w_out_ref[...])


def _prep_first_layer(w_in, w_out):
    return pl.pallas_call(
        _prep_kernel,
        out_shape=(jax.ShapeDtypeStruct((D_MODEL, P_DIM), jnp.bfloat16),
                   jax.ShapeDtypeStruct((D_MODEL, D_MODEL), jnp.bfloat16)),
        grid=(D_MODEL // PREP_ROWS,),
        in_specs=[pl.BlockSpec((None, PREP_ROWS, IN_DIM), lambda i: (0, i, 0)),
                  pl.BlockSpec((None, PREP_ROWS, D_MODEL), lambda i: (0, i, 0))],
        out_specs=(pl.BlockSpec((PREP_ROWS, P_DIM), lambda i: (i, 0)),
                   pl.BlockSpec((PREP_ROWS, D_MODEL), lambda i: (i, 0))),
        compiler_params=pltpu.CompilerParams(dimension_semantics=("arbitrary",)),
        name="prep",
    )(w_in, w_out)


def _ffn_kernel(*refs, final_norm, prep_next):
    if prep_next:
        (x_ref, nrm_ref, wg_ref, wu_ref, wd_ref, fin_ref, w_in_ref, w_out_ref,
         out_ref, w_pack_ref, wo16_ref) = refs
        w_pack_ref[...] = _pack_projection(w_in_ref[...])
        wo16_ref[...] = _bf16(w_out_ref[...])
    else:
        x_ref, nrm_ref, wg_ref, wu_ref, wd_ref, fin_ref, out_ref = refs
    x = x_ref[...]
    xn = _bf16(x * lax.rsqrt(jnp.mean(x * x, axis=-1, keepdims=True) + EPS) * nrm_ref[...])
    act = _bf16(_silu(_dot(xn, wg_ref[...])) * _dot(xn, wu_ref[...]))
    y = x + _dot(act, wd_ref[...])
    if final_norm:
        y = y * lax.rsqrt(jnp.mean(y * y, axis=-1, keepdims=True) + EPS) * fin_ref[...]
    out_ref[...] = y


def _ffn(h2, layer, nrm, wg, wu, wd, fin, final_norm, w_in=None, w_out=None):
    n = h2.shape[0]
    n_steps = n // FFN_ROWS
    prep_next = w_in is not None
    resident = lambda shape: pl.BlockSpec(shape, lambda i: (0, 0), pipeline_mode=pl.Buffered(1))
    row_spec = pl.BlockSpec((FFN_ROWS, D_MODEL), lambda i: (i, 0))
    in_specs = [
        row_spec,
        _layer_spec((1, D_MODEL), layer, 1),
        resident((D_MODEL, D_FF)),
        resident((D_MODEL, D_FF)),
        resident((D_FF, D_MODEL)),
        pl.BlockSpec((1, D_MODEL), lambda i: (0, 0)),
    ]
    args = [h2, nrm, wg, wu, wd, fin]
    out_shape = jax.ShapeDtypeStruct(h2.shape, h2.dtype)
    out_specs = row_spec
    if prep_next:
        slab = D_MODEL // n_steps
        assert slab * n_steps == D_MODEL and slab % BF16_SUBLANES == 0
        in_specs += [pl.BlockSpec((None, slab, IN_DIM), lambda i: (layer + 1, i, 0)),
                     pl.BlockSpec((None, slab, D_MODEL), lambda i: (layer + 1, i, 0))]
        args += [w_in, w_out]
        out_shape = (out_shape, jax.ShapeDtypeStruct((D_MODEL, P_DIM), jnp.bfloat16),
                     jax.ShapeDtypeStruct((D_MODEL, D_MODEL), jnp.bfloat16))
        out_specs = (row_spec, pl.BlockSpec((slab, P_DIM), lambda i: (i, 0)),
                     pl.BlockSpec((slab, D_MODEL), lambda i: (i, 0)))
    return pl.pallas_call(
        functools.partial(_ffn_kernel, final_norm=final_norm, prep_next=prep_next),
        out_shape=out_shape,
        grid=(n_steps,),
        in_specs=in_specs,
        out_specs=out_specs,
        compiler_params=pltpu.CompilerParams(
            dimension_semantics=("arbitrary",),
            vmem_limit_bytes=VMEM_LIMIT_BYTES),
        name="ffn",
    )(*args)


def _pad_lanes(v, offset):
    return jnp.pad(v, ((0, 0), (offset, LANES - offset - v.shape[1])))[:, None, :]


def kernel(x, norm_mix, w_in, conv_w, a_log, dt_bias, o_norm_g, ln_v_g, ln_v_b, w_s, b_s, w_out,
           norm_ffn, w_gate, w_up, w_down, norm_final):
    batch, seq, d = x.shape
    depth = w_in.shape[0]
    w_in16 = _bf16(w_in)
    w_pack, wo16 = _prep_first_layer(w_in16, w_out)
    bs_b = jnp.broadcast_to(b_s[:, :, :, None], (depth, GM_GROUPS, GM_CHUNK, GM_GROUP_DIM))
    alog_p = _pad_lanes(a_log, DN_HEADS)
    dtb_p = _pad_lanes(dt_bias, DN_HEADS)
    row = lambda a: a[:, None, :]
    h = x
    for l in range(depth):
        h, wg16, wu16, wd16 = _mixer(h, l, row(norm_mix), w_pack, conv_w, alog_p, dtb_p, row(o_norm_g),
                                     row(ln_v_g), row(ln_v_b), w_s, bs_b, wo16, w_gate, w_up, w_down)
        h2 = h.reshape(batch * seq, d)
        if l + 1 < depth:
            h2, w_pack, wo16 = _ffn(h2, l, row(norm_ffn), wg16, wu16, wd16, norm_final[None, :], False,
                                    w_in16, w_out)
        else:
            h2 = _ffn(h2, l, row(norm_ffn), wg16, wu16, wd16, norm_final[None, :], True)
        h = h2.reshape(batch, seq, d)
    return h
```
